```python
import math
import jax, jax.numpy as jnp
from jax import lax
import numpy as np

D_MODEL = 2048
BATCH = 4
SEQ = 8192
DEPTH = 2

D_SSM = D_MODEL // 4
SSM_GROUP = 16
N_SSM_GROUPS = D_SSM // SSM_GROUP
SSM_STATE = 64
D_SG = 3 * D_MODEL // 8
SG_HEADS = 8
SG_HEAD_DIM = D_SG // SG_HEADS
SG_CHUNK = 128
D_CONV = 3 * D_MODEL // 8
CONV_GROUPS = 8
CONV_WIDTH = 3
N_BRANCHES = 3
SPLIT_SIZES = (D_SSM, D_SG, D_SG, D_CONV, D_CONV, D_CONV)
SPLIT_POINTS = tuple(int(s) for s in np.cumsum(SPLIT_SIZES))
D_IN_PROJ = sum(SPLIT_SIZES) + N_BRANCHES * D_MODEL

FFN_DIM = 256 * ((8 * D_MODEL // 3 + 255) // 256)
N_EXPERTS = 8
TOP_K = 2
EXPERT_DIM = 7 * D_MODEL // 2
MOE_BLOCK = 512
N_DENSE = (DEPTH + 1) // 2
N_MOE = DEPTH // 2

DEEPNORM_ALPHA = (2.0 * DEPTH) ** 0.25
DEEPNORM_BETA = (8.0 * DEPTH) ** -0.25
LN_EPS = 1e-5

kernel_name = "hybrid_ssm_gmlp_shortconv_moe_deepnorm_adaln"


def layer_norm(z):
    z32 = z.astype(jnp.float32)
    mu = jnp.mean(z32, axis=-1, keepdims=True)
    var = jnp.mean(jnp.square(z32 - mu), axis=-1, keepdims=True)
    return ((z32 - mu) * lax.rsqrt(var + LN_EPS)).astype(z.dtype)


def ssm_branch(u, a_re, a_im, log_dt, b_re, b_im, c_re, c_im, d_skip, glu_w, glu_b):
    bsz, seq, _ = u.shape
    f32 = jnp.float32
    u32 = u.astype(f32).reshape(bsz, seq, N_SSM_GROUPS, SSM_GROUP)
    a_re, a_im = a_re.astype(f32), a_im.astype(f32)
    b_re, b_im = b_re.astype(f32), b_im.astype(f32)
    c_re, c_im = c_re.astype(f32), c_im.astype(f32)
    dt = jnp.exp(log_dt.astype(f32))[:, None]
    mag = jnp.exp(dt * a_re)
    ang = dt * a_im
    abar_re = mag * jnp.cos(ang)
    abar_im = mag * jnp.sin(ang)
    den = jnp.square(a_re) + jnp.square(a_im)
    num_re = abar_re - 1.0
    f_re = (num_re * a_re + abar_im * a_im) / den
    f_im = (abar_im * a_re - num_re * a_im) / den
    bbar_re = f_re[..., None] * b_re - f_im[..., None] * b_im
    bbar_im = f_re[..., None] * b_im + f_im[..., None] * b_re
    bu_re = jnp.einsum('bsgc,gnc->bsgn', u32, bbar_re)
    bu_im = jnp.einsum('bsgc,gnc->bsgn', u32, bbar_im)
    lam_re = jnp.broadcast_to(abar_re, (1, seq, N_SSM_GROUPS, SSM_STATE))
    lam_im = jnp.broadcast_to(abar_im, (1, seq, N_SSM_GROUPS, SSM_STATE))

    def combine(left, right):
        ar1, ai1, br1, bi1 = left
        ar2, ai2, br2, bi2 = right
        return (ar2 * ar1 - ai2 * ai1,
                ar2 * ai1 + ai2 * ar1,
                ar2 * br1 - ai2 * bi1 + br2,
                ar2 * bi1 + ai2 * br1 + bi2)

    _, _, h_re, h_im = lax.associative_scan(combine, (lam_re, lam_im, bu_re, bu_im), axis=1)
    y = (jnp.einsum('bsgn,gcn->bsgc', h_re, c_re)
         - jnp.einsum('bsgn,gcn->bsgc', h_im, c_im)
         + d_skip.astype(f32) * u32)
    y = y.reshape(bsz, seq, D_SSM).astype(u.dtype)
    z = jax.nn.gelu(y)
    return z * jax.nn.sigmoid(z @ glu_w + glu_b)


def spatial_gating_branch(u, v, ln_g, ln_b, w_s, b_s):
    bsz, seq, _ = u.shape
    v = layer_norm(v) * ln_g + ln_b
    v = v.reshape(bsz, seq // SG_CHUNK, SG_CHUNK, SG_HEADS, SG_HEAD_DIM)
    mask = jnp.tril(jnp.ones((SG_CHUNK, SG_CHUNK), dtype=bool))
    w = jnp.where(mask[None], w_s, jnp.zeros_like(w_s))
    s = jnp.einsum('hts,bnshd->bnthd', w, v) + b_s.T[None, None, :, :, None]
    return u * s.reshape(bsz, seq, D_SG)


def short_conv_branch(b_gate, c_gate, h_in, conv_w):
    z = c_gate * h_in
    y = lax.conv_general_dilated(z, conv_w, window_strides=(1,),
                                 padding=[(CONV_WIDTH - 1, 0)],
                                 dimension_numbers=('NWC', 'WIO', 'NWC'),
                                 feature_group_count=D_CONV)
    return b_gate * y


def hybrid_mixer(h, w_in, a_re, a_im, log_dt, b_re, b_im, c_re, c_im, d_skip,
                 glu_w, glu_b, sg_ln_g, sg_ln_b, sg_w, sg_b, conv_w,
                 w_branch_a, w_branch_b, w_branch_c, w_o):
    bsz, seq, _ = h.shape
    proj = h @ w_in
    u_ssm, u_sg, v_sg, b_cv, c_cv, h_cv, gate_logits = jnp.split(proj, SPLIT_POINTS, axis=-1)
    ya = ssm_branch(u_ssm, a_re, a_im, log_dt, b_re, b_im, c_re, c_im, d_skip, glu_w, glu_b) @ w_branch_a
    yb = spatial_gating_branch(u_sg, v_sg, sg_ln_g, sg_ln_b, sg_w, sg_b) @ w_branch_b
    yc = short_conv_branch(b_cv, c_cv, h_cv, conv_w) @ w_branch_c
    gates = jax.nn.sigmoid(gate_logits.reshape(bsz, seq, N_BRANCHES, D_MODEL))
    merged = gates[:, :, 0] * ya + gates[:, :, 1] * yb + gates[:, :, 2] * yc
    return merged @ w_o


def swiglu(h, w13, w2):
    gate, up = jnp.split(h @ w13, 2, axis=-1)
    return (jax.nn.silu(gate) * up) @ w2


def moe_swiglu(h, router_w, router_b, w13, w2):
    bsz, seq, dm = h.shape
    n_tok = bsz * seq
    ht = h.reshape(n_tok, dm)
    logits = (ht @ router_w).astype(jnp.float32) + router_b.astype(jnp.float32)
    top_logits, top_idx = lax.top_k(logits, TOP_K)
    top_w = jax.nn.softmax(top_logits, axis=-1).astype(h.dtype)
    n_pairs = TOP_K * n_tok
    flat_e = top_idx.reshape(-1)
    flat_tok = jnp.repeat(jnp.arange(n_tok, dtype=jnp.int32), TOP_K)
    flat_w = top_w.reshape(-1)
    order = jnp.argsort(flat_e)
    sorted_e = flat_e[order]
    counts = jnp.bincount(flat_e, length=N_EXPERTS)
    padded = ((counts + MOE_BLOCK - 1) // MOE_BLOCK) * MOE_BLOCK
    ends_p = jnp.cumsum(padded)
    starts_p = ends_p - padded
    starts = jnp.cumsum(counts) - counts
    rank = jnp.arange(n_pairs) - starts[sorted_e]
    dest = starts_p[sorted_e] + rank
    buf_len = ((n_pairs + MOE_BLOCK - 1) // MOE_BLOCK) * MOE_BLOCK + N_EXPERTS * MOE_BLOCK
    n_blocks = buf_len // MOE_BLOCK
    buf_tok = jnp.full((buf_len,), n_tok, dtype=jnp.int32).at[dest].set(flat_tok[order])
    buf_w = jnp.zeros((buf_len,), h.dtype).at[dest].set(flat_w[order])
    block_e = jnp.minimum(jnp.searchsorted(ends_p, jnp.arange(n_blocks) * MOE_BLOCK, side='right'),
                          N_EXPERTS - 1)
    x_pad = jnp.concatenate([ht, jnp.zeros((1, dm), ht.dtype)], axis=0)
    xs = x_pad[buf_tok].reshape(n_blocks, MOE_BLOCK, dm)

    def expert_block(args):
        xb, e = args
        return swiglu(xb, w13[e], w2[e])

    ys = lax.map(expert_block, (xs, block_e)).reshape(buf_len, dm)
    out = jax.ops.segment_sum(ys * buf_w[:, None], buf_tok, num_segments=n_tok + 1)[:n_tok]
    return out.reshape(bsz, seq, dm)


def adaln_post_norm(x, y, shift, scale, gate):
    z = DEEPNORM_ALPHA * x + (1.0 + gate)[:, None, :] * y
    return layer_norm(z) * (1.0 + scale)[:, None, :] + shift[:, None, :]


def setup_inputs(seed: int = 0) -> dict:
    key = jax.random.key(seed)
    ks = jax.random.split(key, 32)
    nrm = jax.random.normal
    L, G, N, C = DEPTH, N_SSM_GROUPS, SSM_STATE, SSM_GROUP
    a_im_init = jnp.pi * jnp.arange(N, dtype=jnp.float32)
    return {
        "x": nrm(ks[0], (BATCH, SEQ, D_MODEL), jnp.float32),
        "c": nrm(ks[1], (BATCH, D_MODEL), jnp.float32),
        "w_in": nrm(ks[2], (L, D_MODEL, D_IN_PROJ)) * D_MODEL ** -0.5,
        "ssm_a_re": -0.5 + 0.01 * nrm(ks[3], (L, G, N)),
        "ssm_a_im": a_im_init + 0.01 * nrm(ks[4], (L, G, N)),
        "ssm_log_dt": jax.random.uniform(ks[5], (L, G), minval=math.log(1e-3), maxval=math.log(1e-1)),
        "ssm_b_re": nrm(ks[6], (L, G, N, C)) * (2.0 * C) ** -0.5,
        "ssm_b_im": nrm(ks[7], (L, G, N, C)) * (2.0 * C) ** -0.5,
        "ssm_c_re": nrm(ks[8], (L, G, C, N)) * (2.0 * N) ** -0.5,
        "ssm_c_im": nrm(ks[9], (L, G, C, N)) * (2.0 * N) ** -0.5,
        "ssm_d": nrm(ks[10], (L, G, C)),
        "glu_w": nrm(ks[11], (L, D_SSM, D_SSM)) * D_SSM ** -0.5,
        "glu_b": 0.02 * nrm(ks[12], (L, D_SSM)),
        "sg_ln_g": 1.0 + 0.02 * nrm(ks[13], (L, D_SG)),
        "sg_ln_b": 0.02 * nrm(ks[14], (L, D_SG)),
        "sg_w": nrm(ks[15], (L, SG_HEADS, SG_CHUNK, SG_CHUNK)) * SG_CHUNK ** -0.5,
        "sg_b": 1.0 + 0.02 * nrm(ks[16], (L, SG_HEADS, SG_CHUNK)),
        "conv_w": nrm(ks[17], (L, CONV_WIDTH, 1, D_CONV)) * CONV_WIDTH ** -0.5,
        "w_branch_a": nrm(ks[18], (L, D_SSM, D_MODEL)) * D_SSM ** -0.5,
        "w_branch_b": nrm(ks[19], (L, D_SG, D_MODEL)) * D_SG ** -0.5,
        "w_branch_c": nrm(ks[20], (L, D_CONV, D_MODEL)) * D_CONV ** -0.5,
        "w_o": nrm(ks[21], (L, D_MODEL, D_MODEL)) * (D_MODEL ** -0.5 * DEEPNORM_BETA),
        "ada_w": nrm(ks[22], (L, D_MODEL, 6 * D_MODEL)) * (0.1 * D_MODEL ** -0.5),
        "ada_b": 0.02 * nrm(ks[23], (L, 6 * D_MODEL)),
        "ffn_w13": nrm(ks[24], (N_DENSE, D_MODEL, 2 * FFN_DIM)) * D_MODEL ** -0.5,
        "ffn_w2": nrm(ks[25], (N_DENSE, FFN_DIM, D_MODEL)) * (FFN_DIM ** -0.5 * DEEPNORM_BETA),
        "moe_router_w": nrm(ks[26], (N_MOE, D_MODEL, N_EXPERTS)) * D_MODEL ** -0.5,
        "moe_router_b": 0.01 * nrm(ks[27], (N_MOE, N_EXPERTS)),
        "moe_w13": nrm(ks[28], (N_MOE, N_EXPERTS, D_MODEL, 2 * EXPERT_DIM)) * D_MODEL ** -0.5,
        "moe_w2": nrm(ks[29], (N_MOE, N_EXPERTS, EXPERT_DIM, D_MODEL)) * (EXPERT_DIM ** -0.5 * DEEPNORM_BETA),
    }


def reference(x, c, w_in, ssm_a_re, ssm_a_im, ssm_log_dt, ssm_b_re, ssm_b_im, ssm_c_re, ssm_c_im,
              ssm_d, glu_w, glu_b, sg_ln_g, sg_ln_b, sg_w, sg_b, conv_w,
              w_branch_a, w_branch_b, w_branch_c, w_o, ada_w, ada_b,
              ffn_w13, ffn_w2, moe_router_w, moe_router_b, moe_w13, moe_w2):
    for l in range(DEPTH):
        mod = c @ ada_w[l] + ada_b[l]
        sh_m, sc_m, g_m, sh_f, sc_f, g_f = jnp.split(mod, 6, axis=-1)
        y = hybrid_mixer(x, w_in[l], ssm_a_re[l], ssm_a_im[l], ssm_log_dt[l],
                         ssm_b_re[l], ssm_b_im[l], ssm_c_re[l], ssm_c_im[l], ssm_d[l],
                         glu_w[l], glu_b[l], sg_ln_g[l], sg_ln_b[l], sg_w[l], sg_b[l], conv_w[l],
                         w_branch_a[l], w_branch_b[l], w_branch_c[l], w_o[l])
        x = adaln_post_norm(x, y, sh_m, sc_m, g_m)
        if l % 2 == 0:
            y = swiglu(x, ffn_w13[l // 2], ffn_w2[l // 2])
        else:
            y = moe_swiglu(x, moe_router_w[l // 2], moe_router_b[l // 2], moe_w13[l // 2], moe_w2[l // 2])
        x = adaln_post_norm(x, y, sh_f, sc_f, g_f)
    return x
```

```python
import functools
import math

import jax
import jax.numpy as jnp
from jax import lax
from jax.experimental import pallas as pl
from jax.experimental.pallas import tpu as pltpu

BF16 = jnp.bfloat16
F32 = jnp.float32

LN_EPS = 1e-5
TOP_K = 2
SSM_CHUNK = 32
MOE_ROWS = 512
V7X_VMEM_LIMIT = 56 * 1024 * 1024
HIGHEST = lax.Precision.HIGHEST


def _tile(dim, pref):
    if dim <= pref:
        return dim
    for t in range(pref - pref % 128, 0, -128):
        if dim % t == 0:
            return t
    raise ValueError((dim, pref))


def _params(sem):
    return pltpu.CompilerParams(dimension_semantics=sem, vmem_limit_bytes=V7X_VMEM_LIMIT)


def _dot(a, b):
    return jnp.dot(a, b, preferred_element_type=F32)


def _ada_kernel(c_ref, w_ref, b_ref, o_ref):
    o_ref[...] = jnp.dot(c_ref[...], w_ref[...], preferred_element_type=F32, precision=HIGHEST) + b_ref[...]


def ada_modulation(c, ada_w, ada_b):
    nl, d, d6 = ada_w.shape
    bsz = c.shape[0]
    tn = _tile(d6, 1024)
    return pl.pallas_call(
        _ada_kernel, name="ada_modulation",
        grid=(nl, d6 // tn),
        in_specs=[pl.BlockSpec((bsz, d), lambda l, j: (0, 0)),
                  pl.BlockSpec((None, d, tn), lambda l, j: (l, 0, j)),
                  pl.BlockSpec((None, 1, tn), lambda l, j: (l, 0, j))],
        out_specs=pl.BlockSpec((None, bsz, tn), lambda l, j: (l, 0, j)),
        out_shape=jax.ShapeDtypeStruct((nl, bsz, d6), F32),
        compiler_params=_params(("parallel", "parallel")),
    )(c, ada_w, ada_b.reshape(nl, 1, d6))


def _mm_kernel(a_ref, b_ref, o_ref, *, sigmoid):
    acc = _dot(a_ref[...], b_ref[...])
    if sigmoid:
        acc = jax.nn.sigmoid(acc)
    o_ref[...] = acc.astype(o_ref.dtype)


def matmul(a, b, *, tm, tn, name, out_dtype=BF16, sigmoid=False):
    m, k = a.shape
    n = b.shape[1]
    tm, tn = _tile(m, tm), _tile(n, tn)
    return pl.pallas_call(
        functools.partial(_mm_kernel, sigmoid=sigmoid), name=name,
        grid=(m // tm, n // tn),
        in_specs=[pl.BlockSpec((tm, k), lambda i, j: (i, 0)),
                  pl.BlockSpec((k, tn), lambda i, j: (0, j))],
        out_specs=pl.BlockSpec((tm, tn), lambda i, j: (i, j)),
        out_shape=jax.ShapeDtypeStruct((m, n), out_dtype),
        compiler_params=_params(("parallel", "parallel")),
    )(a, b)


def ssm_matrices(a_re, a_im, log_dt, b_re, b_im, c_re, c_im, d_skip, t_chunk):
    g, n = a_re.shape
    c = b_re.shape[-1]
    dt = jnp.exp(log_dt)[:, None]
    abar_re = jnp.exp(dt * a_re) * jnp.cos(dt * a_im)
    abar_im = jnp.exp(dt * a_re) * jnp.sin(dt * a_im)
    den = jnp.square(a_re) + jnp.square(a_im)
    num_re = abar_re - 1.0
    f_re = (num_re * a_re + abar_im * a_im) / den
    f_im = (abar_im * a_re - num_re * a_im) / den
    bbar_re = f_re[..., None] * b_re - f_im[..., None] * b_im
    bbar_im = f_re[..., None] * b_im + f_im[..., None] * b_re
    k = jnp.arange(t_chunk + 1, dtype=a_re.dtype)[:, None, None]
    pw_mag = jnp.exp(k * (dt * a_re))
    pw_re = pw_mag * jnp.cos(k * (dt * a_im))
    pw_im = pw_mag * jnp.sin(k * (dt * a_im))
    e_re = pw_re[..., None] * bbar_re - pw_im[..., None] * bbar_im
    e_im = pw_re[..., None] * bbar_im + pw_im[..., None] * bbar_re
    kern = (jnp.einsum('gcn,kgnd->gkcd', c_re, e_re[:t_chunk], precision=HIGHEST)
            - jnp.einsum('gcn,kgnd->gkcd', c_im, e_im[:t_chunk], precision=HIGHEST))
    kern = kern.at[:, 0].add(d_skip[:, :, None] * jnp.eye(c, dtype=kern.dtype))
    step = jnp.arange(t_chunk)
    lag = step[None, :] - step[:, None]
    toe = kern[:, jnp.clip(lag, 0, t_chunk - 1)]
    toe = jnp.where((lag >= 0)[None, :, :, None, None], toe, 0.0)
    m_mat = toe.transpose(0, 1, 4, 2, 3).reshape(g, t_chunk * c, t_chunk * c)
    p_re = e_re[:t_chunk][::-1].transpose(1, 0, 3, 2).reshape(g, t_chunk * c, n)
    p_im = e_im[:t_chunk][::-1].transpose(1, 0, 3, 2).reshape(g, t_chunk * c, n)
    q_re = (c_re[None] * pw_re[1:, :, None, :] - c_im[None] * pw_im[1:, :, None, :])
    q_im = -(c_re[None] * pw_im[1:, :, None, :] + c_im[None] * pw_re[1:, :, None, :])
    q_re = q_re.transpose(1, 3, 0, 2).reshape(g, n, t_chunk * c)
    q_im = q_im.transpose(1, 3, 0, 2).reshape(g, n, t_chunk * c)
    lam = jnp.stack([pw_re[t_chunk], pw_im[t_chunk]], axis=1)
    return m_mat, p_re, p_im, q_re, q_im, lam


def _ssm_kernel(x_ref, m_ref, pre_ref, pim_ref, qre_ref, qim_ref, lam_ref, o_ref,
                sre_ref, sim_ref, hre_ref, him_ref, *, n_steps, bsz):
    x = x_ref[...]
    sre_ref[...] = _dot(x, pre_ref[...])
    sim_ref[...] = _dot(x, pim_ref[...])
    lam_re = lam_ref[0:1, :]
    lam_im = lam_ref[1:2, :]
    n_state = lam_re.shape[-1]

    def step(j, carry):
        h_re, h_im = carry
        rows = pl.ds(j * bsz, bsz)
        hre_ref[rows, :] = h_re
        him_ref[rows, :] = h_im
        new_re = lam_re * h_re - lam_im * h_im + sre_ref[rows, :]
        new_im = lam_re * h_im + lam_im * h_re + sim_ref[rows, :]
        return new_re, new_im

    zero = jnp.zeros((bsz, n_state), F32)
    lax.fori_loop(0, n_steps, step, (zero, zero))
    y = _dot(x, m_ref[...])
    y = y + _dot(hre_ref[...].astype(BF16), qre_ref[...])
    y = y + _dot(him_ref[...].astype(BF16), qim_ref[...])
    o_ref[...] = y.astype(o_ref.dtype)


def ssm_scan(u, mats, bsz, seq, t_chunk):
    m_mat, p_re, p_im, q_re, q_im, lam = mats
    g, tc, _ = m_mat.shape
    n = lam.shape[-1]
    c = tc // t_chunk
    n_steps = seq // t_chunk
    rows = n_steps * bsz
    x = u.reshape(bsz, n_steps, t_chunk, g, c).transpose(3, 1, 0, 2, 4).reshape(g, rows, tc)
    wspec = lambda shape: pl.BlockSpec((None,) + shape, lambda i: (i, 0, 0))
    y = pl.pallas_call(
        functools.partial(_ssm_kernel, n_steps=n_steps, bsz=bsz), name="ssm_scan",
        grid=(g,),
        in_specs=[wspec((rows, tc)), wspec((tc, tc)), wspec((tc, n)), wspec((tc, n)),
                  wspec((n, tc)), wspec((n, tc)), wspec((2, n))],
        out_specs=wspec((rows, tc)),
        out_shape=jax.ShapeDtypeStruct((g, rows, tc), BF16),
        scratch_shapes=[pltpu.VMEM((rows, n), F32) for _ in range(4)],
        compiler_params=_params(("parallel",)),
    )(x, m_mat.astype(BF16), p_re.astype(BF16), p_im.astype(BF16),
      q_re.astype(BF16), q_im.astype(BF16), lam)
    return y.reshape(g, n_steps, bsz, t_chunk, c).transpose(2, 1, 3, 0, 4).reshape(bsz * seq, g * c)


def _layer_norm(z):
    mu = jnp.mean(z, axis=-1, keepdims=True)
    zc = z - mu
    var = jnp.mean(zc * zc, axis=-1, keepdims=True)
    return zc * lax.rsqrt(var + LN_EPS)


def _mixer_kernel(ya_ref, u_ref, v_ref, b_ref, c_ref, h_ref, ch_ref, hh_ref, g0_ref, g1_ref, g2_ref,
                  gluw_ref, glub_ref, lng_ref, lnb_ref, sgw_ref, sgb_ref, cw_ref, wa_ref, wb_ref, wc_ref,
                  o_ref, z_ref, vs_ref, zb_ref, *, tm, seq, chunk, heads):
    z = jax.nn.gelu(ya_ref[...].astype(F32), approximate=True)
    za = z * jax.nn.sigmoid(_dot(z.astype(BF16), gluw_ref[...]) + glub_ref[...])
    merged = g0_ref[...].astype(F32) * _dot(za.astype(BF16), wa_ref[...])

    d_sg = v_ref.shape[-1]
    head_dim = d_sg // heads
    v = (_layer_norm(v_ref[...].astype(F32)) * lng_ref[...] + lnb_ref[...]).astype(BF16)
    col_head = lax.broadcasted_iota(jnp.int32, (chunk, d_sg), 1) // head_dim
    for ci in range(tm // chunk):
        vc = v[ci * chunk:(ci + 1) * chunk, :]
        for hd in range(heads):
            vs_ref[hd * chunk:(hd + 1) * chunk, :] = jnp.where(col_head == hd, vc, jnp.zeros_like(vc))
        s = _dot(sgw_ref[...], vs_ref[...]) + sgb_ref[...]
        zb_ref[ci * chunk:(ci + 1) * chunk, :] = (u_ref[ci * chunk:(ci + 1) * chunk, :].astype(F32) * s).astype(BF16)
    merged = merged + g1_ref[...].astype(F32) * _dot(zb_ref[...], wb_ref[...])

    first = (pl.program_id(0) * tm) % seq == 0
    halo = ch_ref[...].astype(F32) * hh_ref[...].astype(F32)
    z_ref[0:8, :] = jnp.where(first, jnp.zeros_like(halo), halo)
    z_ref[8:, :] = c_ref[...].astype(F32) * h_ref[...].astype(F32)
    conv = (cw_ref[0:1, :] * z_ref[pl.ds(6, tm), :] + cw_ref[1:2, :] * z_ref[pl.ds(7, tm), :]
            + cw_ref[2:3, :] * z_ref[pl.ds(8, tm), :])
    zc = b_ref[...].astype(F32) * conv
    merged = merged + g2_ref[...].astype(F32) * _dot(zc.astype(BF16), wc_ref[...])
    o_ref[...] = merged.astype(o_ref.dtype)


def mixer_tail(ya, p5, gates, glu_w, glu_b, ln_g, ln_b, sg_w, sg_b, conv_w, wa, wb, wc, *, seq, tm):
    m, d_ssm = ya.shape
    d_sg = p5.shape[1] // 5
    d_model = wa.shape[1]
    heads, chunk, _ = sg_w.shape
    tm = _tile(m, tm)
    assert tm % chunk == 0 and seq % tm == 0 and conv_w.shape[0] == 3
    w_low = jnp.where(jnp.tril(jnp.ones((chunk, chunk), dtype=bool))[None], sg_w, 0.0)
    sgw_cat = w_low.transpose(1, 0, 2).reshape(chunk, heads * chunk).astype(BF16)
    sgb_full = jnp.repeat(sg_b.T, d_sg // heads, axis=1)
    seg = lambda k: pl.BlockSpec((tm, d_sg), lambda i: (i, k))
    halo = lambda k: pl.BlockSpec((8, d_sg), lambda i: (jnp.maximum(i * (tm // 8) - 1, 0), k))
    gate = lambda k: pl.BlockSpec((tm, d_model), lambda i: (i, k))
    full = lambda a: pl.BlockSpec(a.shape, lambda i: (0,) * a.ndim)
    consts = [glu_w.astype(BF16), glu_b.reshape(1, -1), ln_g.reshape(1, -1), ln_b.reshape(1, -1),
              sgw_cat, sgb_full, conv_w.reshape(3, -1), wa.astype(BF16), wb.astype(BF16), wc.astype(BF16)]
    return pl.pallas_call(
        functools.partial(_mixer_kernel, tm=tm, seq=seq, chunk=chunk, heads=heads), name="mixer_tail",
        grid=(m // tm,),
        in_specs=[pl.BlockSpec((tm, d_ssm), lambda i: (i, 0)),
                  seg(0), seg(1), seg(2), seg(3), seg(4), halo(3), halo(4),
                  gate(0), gate(1), gate(2)] + [full(a) for a in consts],
        out_specs=pl.BlockSpec((tm, d_model), lambda i: (i, 0)),
        out_shape=jax.ShapeDtypeStruct((m, d_model), BF16),
        scratch_shapes=[pltpu.VMEM((tm + 8, d_sg), F32),
                        pltpu.VMEM((heads * chunk, d_sg), BF16),
                        pltpu.VMEM((tm, d_sg), BF16)],
        compiler_params=_params(("parallel",)),
    )(ya, p5, p5, p5, p5, p5, p5, p5, gates, gates, gates, *consts)


def _post_norm(x, y, sh, sc, g, alpha):
    z = alpha * x + (1.0 + g) * y
    return _layer_norm(z) * (1.0 + sc) + sh


def _mod_specs(tm, seq, d):
    spec = pl.BlockSpec((None, 1, d), lambda i, *_: ((i * tm) // seq, 0, 0))
    return [spec, spec, spec]


def _outproj_kernel(a_ref, w_ref, x_ref, sh_ref, sc_ref, g_ref, o32_ref, o16_ref, *, alpha):
    y = _dot(a_ref[...], w_ref[...])
    out = _post_norm(x_ref[...], y, sh_ref[...], sc_ref[...], g_ref[...], alpha)
    o32_ref[...] = out
    o16_ref[...] = out.astype(BF16)


def outproj_postnorm(a, w_o, x, sh, sc, g, *, seq, alpha, tm):
    m, d = x.shape
    tm = _tile(m, tm)
    assert seq % tm == 0
    row = pl.BlockSpec((tm, d), lambda i: (i, 0))
    return pl.pallas_call(
        functools.partial(_outproj_kernel, alpha=alpha), name="outproj_postnorm",
        grid=(m // tm,),
        in_specs=[pl.BlockSpec((tm, a.shape[1]), lambda i: (i, 0)),
                  pl.BlockSpec(w_o.shape, lambda i: (0, 0)), row] + _mod_specs(tm, seq, d),
        out_specs=[row, row],
        out_shape=[jax.ShapeDtypeStruct((m, d), F32), jax.ShapeDtypeStruct((m, d), BF16)],
        compiler_params=_params(("parallel",)),
    )(a, w_o, x, sh, sc, g)


def _swiglu_step(xb, w1_ref, w3_ref, w2_ref, acc_ref, f):
    gate = _dot(xb, w1_ref[...])
    up = _dot(xb, w3_ref[...])
    act = (gate * jax.nn.sigmoid(gate) * up).astype(BF16)
    part = _dot(act, w2_ref[...])

    @pl.when(f == 0)
    def _():
        acc_ref[...] = part

    @pl.when(f > 0)
    def _():
        acc_ref[...] += part


def _ffn_kernel(xb_ref, w1_ref, w3_ref, w2_ref, x_ref, sh_ref, sc_ref, g_ref, o32_ref, o16_ref, acc_ref, *, alpha):
    f = pl.program_id(1)
    _swiglu_step(xb_ref[...], w1_ref, w3_ref, w2_ref, acc_ref, f)

    @pl.when(f == pl.num_programs(1) - 1)
    def _():
        out = _post_norm(x_ref[...], acc_ref[...], sh_ref[...], sc_ref[...], g_ref[...], alpha)
        o32_ref[...] = out
        o16_ref[...] = out.astype(BF16)


def ffn_postnorm(xb, x, w13, w2, sh, sc, g, *, seq, alpha, tm, tf):
    m, d = x.shape
    ffn = w2.shape[0]
    tm, tf = _tile(m, tm), _tile(ffn, tf)
    nf = ffn // tf
    assert seq % tm == 0
    row = pl.BlockSpec((tm, d), lambda i, f: (i, 0))
    return pl.pallas_call(
        functools.partial(_ffn_kernel, alpha=alpha), name="ffn_postnorm",
        grid=(m // tm, nf),
        in_specs=[row,
                  pl.BlockSpec((d, tf), lambda i, f: (0, f)),
                  pl.BlockSpec((d, tf), lambda i, f: (0, nf + f)),
                  pl.BlockSpec((tf, d), lambda i, f: (f, 0)),
                  row] + _mod_specs(tm, seq, d),
        out_specs=[row, row],
        out_shape=[jax.ShapeDtypeStruct((m, d), F32), jax.ShapeDtypeStruct((m, d), BF16)],
        scratch_shapes=[pltpu.VMEM((tm, d), F32)],
        compiler_params=_params(("parallel", "arbitrary")),
    )(xb, w13, w13, w2, x, sh, sc, g)


def _router_kernel(x_ref, rwt_ref, rb_ref, e_ref, w_ref):
    logits = lax.dot_general(rwt_ref[...], x_ref[...], (((1,), (1,)), ((), ())),
                             preferred_element_type=F32, precision=HIGHEST) + rb_ref[...]
    n_exp = logits.shape[0]
    eid = lax.broadcasted_iota(jnp.int32, logits.shape, 0)
    m1 = jnp.max(logits, axis=0, keepdims=True)
    e1 = jnp.min(jnp.where(logits == m1, eid, n_exp), axis=0, keepdims=True)
    rest = jnp.where(eid == e1, -jnp.inf, logits)
    m2 = jnp.max(rest, axis=0, keepdims=True)
    e2 = jnp.min(jnp.where(rest == m2, eid, n_exp), axis=0, keepdims=True)
    ex = jnp.exp(m2 - m1)
    e_ref[0:1, :] = e1
    e_ref[1:2, :] = e2
    w_ref[0:1, :] = 1.0 / (1.0 + ex)
    w_ref[1:2, :] = ex / (1.0 + ex)


def router_top2(x, router_w, router_b, *, tm):
    m, d = x.shape
    n_exp = router_w.shape[1]
    tm = _tile(m, tm)
    out = pl.BlockSpec((TOP_K, tm), lambda i: (0, i))
    return pl.pallas_call(
        _router_kernel, name="router_top2",
        grid=(m // tm,),
        in_specs=[pl.BlockSpec((tm, d), lambda i: (i, 0)),
                  pl.BlockSpec((n_exp, d), lambda i: (0, 0)),
                  pl.BlockSpec((n_exp, 1), lambda i: (0, 0))],
        out_specs=[out, out],
        out_shape=[jax.ShapeDtypeStruct((TOP_K, m), jnp.int32), jax.ShapeDtypeStruct((TOP_K, m), F32)],
        compiler_params=_params(("parallel",)),
    )(x, router_w.T, router_b.reshape(n_exp, 1))


def _rank_kernel(e_ref, rank_ref, count_ref, carry_ref, *, n_exp):
    @pl.when(pl.program_id(0) == 0)
    def _():
        carry_ref[...] = jnp.zeros_like(carry_ref)

    e = e_ref[...]
    tb = e.shape[1]
    eid = lax.broadcasted_iota(jnp.int32, (n_exp, tb), 0)
    hit = (eid == e[0:1, :]) | (eid == e[1:2, :])
    onehot = hit.astype(F32)
    upper = (lax.broadcasted_iota(jnp.int32, (tb, tb), 0) <= lax.broadcasted_iota(jnp.int32, (tb, tb), 1))
    incl = _dot(hit.astype(BF16), upper.astype(BF16))
    excl = incl - onehot + carry_ref[...]
    r0 = jnp.sum(jnp.where(eid == e[0:1, :], excl, 0.0), axis=0, keepdims=True)
    r1 = jnp.sum(jnp.where(eid == e[1:2, :], excl, 0.0), axis=0, keepdims=True)
    rank_ref[0:1, :] = r0.astype(jnp.int32)
    rank_ref[1:2, :] = r1.astype(jnp.int32)
    carry_ref[...] += jnp.sum(onehot, axis=1, keepdims=True)
    count_ref[...] = carry_ref[...].astype(jnp.int32)


def expert_ranks(e_idx, n_exp, *, tb):
    m = e_idx.shape[1]
    tb = _tile(m, tb)
    blk = pl.BlockSpec((TOP_K, tb), lambda i: (0, i))
    return pl.pallas_call(
        functools.partial(_rank_kernel, n_exp=n_exp), name="expert_ranks",
        grid=(m // tb,),
        in_specs=[blk],
        out_specs=[blk, pl.BlockSpec((n_exp, 1), lambda i: (0, 0))],
        out_shape=[jax.ShapeDtypeStruct((TOP_K, m), jnp.int32), jax.ShapeDtypeStruct((n_exp, 1), jnp.int32)],
        scratch_shapes=[pltpu.VMEM((n_exp, 1), F32)],
        compiler_params=_params(("arbitrary",)),
    )(e_idx)


def _dispatch_kernel(dest_ref, x_ref, zeros_ref, xs_ref, sem):
    del zeros_ref
    tm = x_ref.shape[0]

    def row_copy(r, k):
        return pltpu.make_async_copy(x_ref.at[pl.ds(r, 1), :], xs_ref.at[pl.ds(dest_ref[0, k, r], 1), :], sem)

    def issue(r, carry):
        for k in range(TOP_K):
            row_copy(r, k).start()
        return carry

    lax.fori_loop(0, tm, issue, 0)

    def drain(r, carry):
        for k in range(TOP_K):
            row_copy(r, k).wait()
        return carry

    lax.fori_loop(0, tm, drain, 0)


def dispatch_rows(x, dest, n_rows, *, tm):
    m, d = x.shape
    tm = _tile(m, tm)
    dest_t = dest.reshape(TOP_K, m // tm, tm).transpose(1, 0, 2)
    return pl.pallas_call(
        _dispatch_kernel, name="dispatch_rows",
        grid=(m // tm,),
        in_specs=[pl.BlockSpec((1, TOP_K, tm), lambda i: (i, 0, 0), memory_space=pltpu.SMEM),
                  pl.BlockSpec((tm, d), lambda i: (i, 0)),
                  pl.BlockSpec(memory_space=pl.ANY)],
        out_specs=pl.BlockSpec(memory_space=pl.ANY),
        out_shape=jax.ShapeDtypeStruct((n_rows, d), x.dtype),
        scratch_shapes=[pltpu.SemaphoreType.DMA(())],
        input_output_aliases={2: 0},
        compiler_params=_params(("arbitrary",)),
    )(dest_t, x, jnp.zeros((n_rows, d), x.dtype))


def _expert_kernel(be_ref, nv_ref, rb_ref, xs_ref, w1_ref, w3_ref, w2_ref, ys_ref, xb_ref, acc_ref):
    del be_ref, rb_ref
    b, f = pl.program_id(0), pl.program_id(1)
    used = nv_ref[b] > 0

    @pl.when(used & (f == 0))
    def _():
        xb_ref[...] = xs_ref[...].astype(BF16)

    @pl.when(used)
    def _():
        _swiglu_step(xb_ref[...], w1_ref, w3_ref, w2_ref, acc_ref, f)

    @pl.when(used & (f == pl.num_programs(1) - 1))
    def _():
        ys_ref[...] = acc_ref[...]

    @pl.when(jnp.logical_not(used) & (f == pl.num_programs(1) - 1))
    def _():
        ys_ref[...] = jnp.zeros_like(ys_ref)


def expert_blocks(xs, w13, w2, block_e, block_nv, block_row, *, rows, tf):
    n_rows, d = xs.shape
    n_exp, ffn, _ = w2.shape
    tf = _tile(ffn, tf)
    nf = ffn // tf
    nb = n_rows // rows
    fi = lambda b, f, nv: jnp.where(nv[b] > 0, f, nf - 1)
    grid_spec = pltpu.PrefetchScalarGridSpec(
        num_scalar_prefetch=3,
        grid=(nb, nf),
        in_specs=[pl.BlockSpec((rows, d), lambda b, f, be, nv, rb: (rb[b], 0)),
                  pl.BlockSpec((None, d, tf), lambda b, f, be, nv, rb: (be[b], 0, fi(b, f, nv))),
                  pl.BlockSpec((None, d, tf), lambda b, f, be, nv, rb: (be[b], 0, nf + fi(b, f, nv))),
                  pl.BlockSpec((None, tf, d), lambda b, f, be, nv, rb: (be[b], fi(b, f, nv), 0))],
        out_specs=pl.BlockSpec((rows, d), lambda b, f, be, nv, rb: (b, 0)),
        scratch_shapes=[pltpu.VMEM((rows, d), BF16), pltpu.VMEM((rows, d), F32)])
    return pl.pallas_call(
        _expert_kernel, name="expert_blocks",
        grid_spec=grid_spec,
        out_shape=jax.ShapeDtypeStruct((n_rows, d), F32),
        compiler_params=_params(("arbitrary", "arbitrary")),
    )(block_e, block_nv, block_row, xs, w13, w13, w2)


def _combine_kernel(dest_ref, ys_ref, w0_ref, w1_ref, x_ref, sh_ref, sc_ref, g_ref, o_ref, buf_ref, sem, *, alpha):
    tm = x_ref.shape[0]

    def row_copy(r, k):
        return pltpu.make_async_copy(ys_ref.at[pl.ds(dest_ref[0, k, r], 1), :], buf_ref.at[k, pl.ds(r, 1), :], sem)

    def issue(r, carry):
        for k in range(TOP_K):
            row_copy(r, k).start()
        return carry

    lax.fori_loop(0, tm, issue, 0)

    def drain(r, carry):
        for k in range(TOP_K):
            row_copy(r, k).wait()
        return carry

    lax.fori_loop(0, tm, drain, 0)
    y = w0_ref[...] * buf_ref[0] + w1_ref[...] * buf_ref[1]
    o_ref[...] = _post_norm(x_ref[...], y, sh_ref[...], sc_ref[...], g_ref[...], alpha)


def combine_postnorm(ys, dest, top_w, x, sh, sc, g, *, seq, alpha, tm):
    m, d = x.shape
    tm = _tile(m, tm)
    assert seq % tm == 0
    dest_t = dest.reshape(TOP_K, m // tm, tm).transpose(1, 0, 2)
    row = pl.BlockSpec((tm, d), lambda i: (i, 0))
    col = pl.BlockSpec((tm, 1), lambda i: (i, 0))
    return pl.pallas_call(
        functools.partial(_combine_kernel, alpha=alpha), name="combine_postnorm",
        grid=(m // tm,),
        in_specs=[pl.BlockSpec((1, TOP_K, tm), lambda i: (i, 0, 0), memory_space=pltpu.SMEM),
                  pl.BlockSpec(memory_space=pl.ANY), col, col, row] + _mod_specs(tm, seq, d),
        out_specs=row,
        out_shape=jax.ShapeDtypeStruct((m, d), F32),
        scratch_shapes=[pltpu.VMEM((TOP_K, tm, d), F32), pltpu.SemaphoreType.DMA(())],
        compiler_params=_params(("arbitrary",)),
    )(dest_t, ys, top_w[0].reshape(m, 1), top_w[1].reshape(m, 1), x, sh, sc, g)


def moe_postnorm(x, router_w, router_b, w13, w2, sh, sc, g, *, seq, alpha):
    m, d = x.shape
    n_exp = router_w.shape[1]
    e_idx, top_w = router_top2(x, router_w, router_b, tm=1024)
    rank, counts = expert_ranks(e_idx, n_exp, tb=512)
    counts = counts[:, 0]
    padded = ((counts + MOE_ROWS - 1) // MOE_ROWS) * MOE_ROWS
    ends = jnp.cumsum(padded)
    starts = ends - padded
    dest = starts[e_idx] + rank
    n_rows = ((TOP_K * m + MOE_ROWS - 1) // MOE_ROWS) * MOE_ROWS + n_exp * MOE_ROWS
    nb = n_rows // MOE_ROWS
    block_start = jnp.arange(nb, dtype=jnp.int32) * MOE_ROWS
    last_used = jnp.maximum(ends[-1] // MOE_ROWS - 1, 0)
    block_e = jnp.minimum(jnp.searchsorted(ends, block_start, side='right'), n_exp - 1).astype(jnp.int32)
    block_e = jnp.where(block_start < ends[-1], block_e, block_e[last_used])
    block_nv = jnp.clip(counts[block_e] - (block_start - starts[block_e]), 0, MOE_ROWS)
    block_nv = jnp.where(block_start < ends[-1], block_nv, 0).astype(jnp.int32)
    block_row = jnp.minimum(jnp.arange(nb, dtype=jnp.int32), last_used).astype(jnp.int32)
    xs = dispatch_rows(x, dest, n_rows, tm=256)
    ys = expert_blocks(xs, w13, w2, block_e, block_nv, block_row, rows=MOE_ROWS, tf=512)
    return combine_postnorm(ys, dest, top_w, x, sh, sc, g, seq=seq, alpha=alpha, tm=256)


def kernel(x, c, w_in, ssm_a_re, ssm_a_im, ssm_log_dt, ssm_b_re, ssm_b_im, ssm_c_re, ssm_c_im, ssm_d, glu_w, glu_b, sg_ln_g, sg_ln_b, sg_w, sg_b, conv_w, w_branch_a, w_branch_b, w_branch_c, w_o, ada_w, ada_b, ffn_w13, ffn_w2, moe_router_w, moe_router_b, moe_w13, moe_w2):
    bsz, seq, d = x.shape
    depth = w_in.shape[0]
    m = bsz * seq
    d_ssm = glu_w.shape[1]
    d_sg = sg_ln_g.shape[1]
    alpha = (2.0 * depth) ** 0.25
    assert conv_w.shape[-1] == d_sg and w_in.shape[2] == d_ssm + 5 * d_sg + 3 * d

    mod = ada_modulation(c, ada_w, ada_b)
    x32 = x.reshape(m, d)
    xb = x32.astype(BF16)
    for l in range(depth):
        sh_m, sc_m, g_m, sh_f, sc_f, g_f = [mod[l, :, k * d:(k + 1) * d].reshape(bsz, 1, d) for k in range(6)]
        w_l = w_in[l].astype(BF16)
        u_ssm = matmul(xb, w_l[:, :d_ssm], tm=1024, tn=512, name="inproj_ssm")
        p5 = matmul(xb, w_l[:, d_ssm:d_ssm + 5 * d_sg], tm=1024, tn=d_sg, name="inproj_gmlp_conv")
        gates = matmul(xb, w_l[:, d_ssm + 5 * d_sg:], tm=1024, tn=1024, sigmoid=True, name="inproj_gates")
        mats = ssm_matrices(ssm_a_re[l], ssm_a_im[l], ssm_log_dt[l], ssm_b_re[l], ssm_b_im[l],
                            ssm_c_re[l], ssm_c_im[l], ssm_d[l], SSM_CHUNK)
        y_ssm = ssm_scan(u_ssm, mats, bsz, seq, SSM_CHUNK)
        merged = mixer_tail(y_ssm, p5, gates, glu_w[l], glu_b[l], sg_ln_g[l], sg_ln_b[l], sg_w[l], sg_b[l],
                            conv_w[l], w_branch_a[l], w_branch_b[l], w_branch_c[l], seq=seq, tm=256)
        x32, xb = outproj_postnorm(merged, w_o[l].astype(BF16), x32, sh_m, sc_m, g_m, seq=seq, alpha=alpha, tm=512)
        if l % 2 == 0:
            x32, xb = ffn_postnorm(xb, x32, ffn_w13[l // 2].astype(BF16), ffn_w2[l // 2].astype(BF16),
                                   sh_f, sc_f, g_f, seq=seq, alpha=alpha, tm=512, tf=512)
        else:
            x32 = moe_postnorm(x32, moe_router_w[l // 2], moe_router_b[l // 2],
                               moe_w13[l // 2].astype(BF16), moe_w2[l // 2].astype(BF16),
                               sh_f, sc_f, g_f, seq=seq, alpha=alpha)
            xb = x32.astype(BF16)
    return x32.reshape(bsz, seq, d)
```

```python
import functools
import math

import jax
import jax.numpy as jnp
from jax import lax
from jax.experimental import pallas as pl
from jax.experimental.pallas import tpu as pltpu

BF16 = jnp.bfloat16
F32 = jnp.float32

LN_EPS = 1e-5
TOP_K = 2
SSM_CHUNK = 32
MOE_ROWS = 512
V7X_VMEM_LIMIT = 56 * 1024 * 1024
HIGHEST = lax.Precision.HIGHEST


def _tile(dim, pref):
    if dim <= pref:
        return dim
    for t in range(pref - pref % 128, 0, -128):
        if dim % t == 0:
            return t
    raise ValueError((dim, pref))


def _params(sem):
    return pltpu.CompilerParams(dimension_semantics=sem, vmem_limit_bytes=V7X_VMEM_LIMIT)


def _dot(a, b):
    return jnp.dot(a, b, preferred_element_type=F32)


def _ada_kernel(c_ref, w_ref, b_ref, o_ref):
    o_ref[...] = jnp.dot(c_ref[...], w_ref[...], preferred_element_type=F32, precision=HIGHEST) + b_ref[...]


def ada_modulation(c, ada_w, ada_b):
    nl, d, d6 = ada_w.shape
    bsz = c.shape[0]
    tn = _tile(d6, 1024)
    return pl.pallas_call(
        _ada_kernel, name="ada_modulation",
        grid=(nl, d6 // tn),
        in_specs=[pl.BlockSpec((bsz, d), lambda l, j: (0, 0)),
                  pl.BlockSpec((None, d, tn), lambda l, j: (l, 0, j)),
                  pl.BlockSpec((None, 1, tn), lambda l, j: (l, 0, j))],
        out_specs=pl.BlockSpec((None, bsz, tn), lambda l, j: (l, 0, j)),
        out_shape=jax.ShapeDtypeStruct((nl, bsz, d6), F32),
        compiler_params=_params(("parallel", "parallel")),
    )(c, ada_w, ada_b.reshape(nl, 1, d6))


def _mm_kernel(a_ref, b_ref, o_ref, *, sigmoid):
    acc = _dot(a_ref[...], b_ref[...])
    if sigmoid:
        acc = jax.nn.sigmoid(acc)
    o_ref[...] = acc.astype(o_ref.dtype)


def matmul(a, b, *, tm, tn, name, out_dtype=BF16, sigmoid=False):
    m, k = a.shape
    n = b.shape[1]
    tm, tn = _tile(m, tm), _tile(n, tn)
    return pl.pallas_call(
        functools.partial(_mm_kernel, sigmoid=sigmoid), name=name,
        grid=(m // tm, n // tn),
        in_specs=[pl.BlockSpec((tm, k), lambda i, j: (i, 0)),
                  pl.BlockSpec((k, tn), lambda i, j: (0, j))],
        out_specs=pl.BlockSpec((tm, tn), lambda i, j: (i, j)),
        out_shape=jax.ShapeDtypeStruct((m, n), out_dtype),
        compiler_params=_params(("parallel", "parallel")),
    )(a, b)


def ssm_matrices(a_re, a_im, log_dt, b_re, b_im, c_re, c_im, d_skip, t_chunk):
    g, n = a_re.shape
    c = b_re.shape[-1]
    dt = jnp.exp(log_dt)[:, None]
    abar_re = jnp.exp(dt * a_re) * jnp.cos(dt * a_im)
    abar_im = jnp.exp(dt * a_re) * jnp.sin(dt * a_im)
    den = jnp.square(a_re) + jnp.square(a_im)
    num_re = abar_re - 1.0
    f_re = (num_re * a_re + abar_im * a_im) / den
    f_im = (abar_im * a_re - num_re * a_im) / den
    bbar_re = f_re[..., None] * b_re - f_im[..., None] * b_im
    bbar_im = f_re[..., None] * b_im + f_im[..., None] * b_re
    k = jnp.arange(t_chunk + 1, dtype=a_re.dtype)[:, None, None]
    pw_mag = jnp.exp(k * (dt * a_re))
    pw_re = pw_mag * jnp.cos(k * (dt * a_im))
    pw_im = pw_mag * jnp.sin(k * (dt * a_im))
    e_re = pw_re[..., None] * bbar_re - pw_im[..., None] * bbar_im
    e_im = pw_re[..., None] * bbar_im + pw_im[..., None] * bbar_re
    kern = (jnp.einsum('gcn,kgnd->gkcd', c_re, e_re[:t_chunk], precision=HIGHEST)
            - jnp.einsum('gcn,kgnd->gkcd', c_im, e_im[:t_chunk], precision=HIGHEST))
    kern = kern.at[:, 0].add(d_skip[:, :, None] * jnp.eye(c, dtype=kern.dtype))
    step = jnp.arange(t_chunk)
    lag = step[None, :] - step[:, None]
    toe = kern[:, jnp.clip(lag, 0, t_chunk - 1)]
    toe = jnp.where((lag >= 0)[None, :, :, None, None], toe, 0.0)
    m_mat = toe.transpose(0, 1, 4, 2, 3).reshape(g, t_chunk * c, t_chunk * c)
    p_re = e_re[:t_chunk][::-1].transpose(1, 0, 3, 2).reshape(g, t_chunk * c, n)
    p_im = e_im[:t_chunk][::-1].transpose(1, 0, 3, 2).reshape(g, t_chunk * c, n)
    q_re = (c_re[None] * pw_re[1:, :, None, :] - c_im[None] * pw_im[1:, :, None, :])
    q_im = -(c_re[None] * pw_im[1:, :, None, :] + c_im[None] * pw_re[1:, :, None, :])
    q_re = q_re.transpose(1, 3, 0, 2).reshape(g, n, t_chunk * c)
    q_im = q_im.transpose(1, 3, 0, 2).reshape(g, n, t_chunk * c)
    lam = jnp.stack([pw_re[t_chunk], pw_im[t_chunk]], axis=1)
    return m_mat, p_re, p_im, q_re, q_im, lam


def _ssm_kernel(x_ref, m_ref, pre_ref, pim_ref, qre_ref, qim_ref, lam_ref, o_ref,
                sre_ref, sim_ref, hre_ref, him_ref, *, n_steps, bsz):
    x = x_ref[...]
    sre_ref[...] = _dot(x, pre_ref[...])
    sim_ref[...] = _dot(x, pim_ref[...])
    lam_re = lam_ref[0:1, :]
    lam_im = lam_ref[1:2, :]
    n_state = lam_re.shape[-1]

    def step(j, carry):
        h_re, h_im = carry
        rows = pl.ds(j * bsz, bsz)
        hre_ref[rows, :] = h_re
        him_ref[rows, :] = h_im
        new_re = lam_re * h_re - lam_im * h_im + sre_ref[rows, :]
        new_im = lam_re * h_im + lam_im * h_re + sim_ref[rows, :]
        return new_re, new_im

    zero = jnp.zeros((bsz, n_state), F32)
    lax.fori_loop(0, n_steps, step, (zero, zero))
    y = _dot(x, m_ref[...])
    y = y + _dot(hre_ref[...].astype(BF16), qre_ref[...])
    y = y + _dot(him_ref[...].astype(BF16), qim_ref[...])
    o_ref[...] = y.astype(o_ref.dtype)


def ssm_scan(u, mats, bsz, seq, t_chunk):
    m_mat, p_re, p_im, q_re, q_im, lam = mats
    g, tc, _ = m_mat.shape
    n = lam.shape[-1]
    c = tc // t_chunk
    n_steps = seq // t_chunk
    rows = n_steps * bsz
    x = u.reshape(bsz, n_steps, t_chunk, g, c).transpose(3, 1, 0, 2, 4).reshape(g, rows, tc)
    wspec = lambda shape: pl.BlockSpec((None,) + shape, lambda i: (i, 0, 0))
    y = pl.pallas_call(
        functools.partial(_ssm_kernel, n_steps=n_steps, bsz=bsz), name="ssm_scan",
        grid=(g,),
        in_specs=[wspec((rows, tc)), wspec((tc, tc)), wspec((tc, n)), wspec((tc, n)),
                  wspec((n, tc)), wspec((n, tc)), wspec((2, n))],
        out_specs=wspec((rows, tc)),
        out_shape=jax.ShapeDtypeStruct((g, rows, tc), BF16),
        scratch_shapes=[pltpu.VMEM((rows, n), F32) for _ in range(4)],
        compiler_params=_params(("parallel",)),
    )(x, m_mat.astype(BF16), p_re.astype(BF16), p_im.astype(BF16),
      q_re.astype(BF16), q_im.astype(BF16), lam)
    return y.reshape(g, n_steps, bsz, t_chunk, c).transpose(2, 1, 3, 0, 4).reshape(bsz * seq, g * c)


def _layer_norm(z):
    mu = jnp.mean(z, axis=-1, keepdims=True)
    zc = z - mu
    var = jnp.mean(zc * zc, axis=-1, keepdims=True)
    return zc * lax.rsqrt(var + LN_EPS)


def _mixer_kernel(ya_ref, u_ref, v_ref, b_ref, c_ref, h_ref, ch_ref, hh_ref, g0_ref, g1_ref, g2_ref,
                  gluw_ref, glub_ref, lng_ref, lnb_ref, sgw_ref, sgb_ref, cw_ref, wa_ref, wb_ref, wc_ref,
                  o_ref, z_ref, vs_ref, zb_ref, *, tm, seq, chunk, heads):
    z = jax.nn.gelu(ya_ref[...].astype(F32), approximate=True)
    za = z * jax.nn.sigmoid(_dot(z.astype(BF16), gluw_ref[...]) + glub_ref[...])
    merged = g0_ref[...].astype(F32) * _dot(za.astype(BF16), wa_ref[...])

    d_sg = v_ref.shape[-1]
    head_dim = d_sg // heads
    v = (_layer_norm(v_ref[...].astype(F32)) * lng_ref[...] + lnb_ref[...]).astype(BF16)
    col_head = lax.broadcasted_iota(jnp.int32, (chunk, d_sg), 1) // head_dim
    for ci in range(tm // chunk):
        vc = v[ci * chunk:(ci + 1) * chunk, :]
        for hd in range(heads):
            vs_ref[hd * chunk:(hd + 1) * chunk, :] = jnp.where(col_head == hd, vc, jnp.zeros_like(vc))
        s = _dot(sgw_ref[...], vs_ref[...]) + sgb_ref[...]
        zb_ref[ci * chunk:(ci + 1) * chunk, :] = (u_ref[ci * chunk:(ci + 1) * chunk, :].astype(F32) * s).astype(BF16)
    merged = merged + g1_ref[...].astype(F32) * _dot(zb_ref[...], wb_ref[...])

    first = (pl.program_id(0) * tm) % seq == 0
    halo = ch_ref[...].astype(F32) * hh_ref[...].astype(F32)
    z_ref[0:8, :] = jnp.where(first, jnp.zeros_like(halo), halo)
    z_ref[8:, :] = c_ref[...].astype(F32) * h_ref[...].astype(F32)
    conv = (cw_ref[0:1, :] * z_ref[pl.ds(6, tm), :] + cw_ref[1:2, :] * z_ref[pl.ds(7, tm), :]
            + cw_ref[2:3, :] * z_ref[pl.ds(8, tm), :])
    zc = b_ref[...].astype(F32) * conv
    merged = merged + g2_ref[...].astype(F32) * _dot(zc.astype(BF16), wc_ref[...])
    o_ref[...] = merged.astype(o_ref.dtype)


def mixer_tail(ya, p5, gates, glu_w, glu_b, ln_g, ln_b, sg_w, sg_b, conv_w, wa, wb, wc, *, seq, tm):
    m, d_ssm = ya.shape
    d_sg = p5.shape[1] // 5
    d_model = wa.shape[1]
    heads, chunk, _ = sg_w.shape
    tm = _tile(m, tm)
    assert tm % chunk == 0 and seq % tm == 0 and conv_w.shape[0] == 3
    w_low = jnp.where(jnp.tril(jnp.ones((chunk, chunk), dtype=bool))[None], sg_w, 0.0)
    sgw_cat = w_low.transpose(1, 0, 2).reshape(chunk, heads * chunk).astype(BF16)
    sgb_full = jnp.repeat(sg_b.T, d_sg // heads, axis=1)
    seg = lambda k: pl.BlockSpec((tm, d_sg), lambda i: (i, k))
    halo = lambda k: pl.BlockSpec((8, d_sg), lambda i: (jnp.maximum(i * (tm // 8) - 1, 0), k))
    gate = lambda k: pl.BlockSpec((tm, d_model), lambda i: (i, k))
    full = lambda a: pl.BlockSpec(a.shape, lambda i: (0,) * a.ndim)
    consts = [glu_w.astype(BF16), glu_b.reshape(1, -1), ln_g.reshape(1, -1), ln_b.reshape(1, -1),
              sgw_cat, sgb_full, conv_w.reshape(3, -1), wa.astype(BF16), wb.astype(BF16), wc.astype(BF16)]
    return pl.pallas_call(
        functools.partial(_mixer_kernel, tm=tm, seq=seq, chunk=chunk, heads=heads), name="mixer_tail",
        grid=(m // tm,),
        in_specs=[pl.BlockSpec((tm, d_ssm), lambda i: (i, 0)),
                  seg(0), seg(1), seg(2), seg(3), seg(4), halo(3), halo(4),
                  gate(0), gate(1), gate(2)] + [full(a) for a in consts],
        out_specs=pl.BlockSpec((tm, d_model), lambda i: (i, 0)),
        out_shape=jax.ShapeDtypeStruct((m, d_model), BF16),
        scratch_shapes=[pltpu.VMEM((tm + 8, d_sg), F32),
                        pltpu.VMEM((heads * chunk, d_sg), BF16),
                        pltpu.VMEM((tm, d_sg), BF16)],
        compiler_params=_params(("parallel",)),
    )(ya, p5, p5, p5, p5, p5, p5, p5, gates, gates, gates, *consts)


def _post_norm(x, y, sh, sc, g, alpha):
    z = alpha * x + (1.0 + g) * y
    return _layer_norm(z) * (1.0 + sc) + sh


def _mod_specs(tm, seq, d):
    spec = pl.BlockSpec((None, 1, d), lambda i, *_: ((i * tm) // seq, 0, 0))
    return [spec, spec, spec]


def _outproj_kernel(a_ref, w_ref, x_ref, sh_ref, sc_ref, g_ref, o32_ref, o16_ref, *, alpha):
    y = _dot(a_ref[...], w_ref[...])
    out = _post_norm(x_ref[...], y, sh_ref[...], sc_ref[...], g_ref[...], alpha)
    o32_ref[...] = out
    o16_ref[...] = out.astype(BF16)


def outproj_postnorm(a, w_o, x, sh, sc, g, *, seq, alpha, tm):
    m, d = x.shape
    tm = _tile(m, tm)
    assert seq % tm == 0
    row = pl.BlockSpec((tm, d), lambda i: (i, 0))
    return pl.pallas_call(
        functools.partial(_outproj_kernel, alpha=alpha), name="outproj_postnorm",
        grid=(m // tm,),
        in_specs=[pl.BlockSpec((tm, a.shape[1]), lambda i: (i, 0)),
                  pl.BlockSpec(w_o.shape, lambda i: (0, 0)), row] + _mod_specs(tm, seq, d),
        out_specs=[row, row],
        out_shape=[jax.ShapeDtypeStruct((m, d), F32), jax.ShapeDtypeStruct((m, d), BF16)],
        compiler_params=_params(("parallel",)),
    )(a, w_o, x, sh, sc, g)


def _swiglu_step(xb, w1_ref, w3_ref, w2_ref, acc_ref):
    gate = _dot(xb, w1_ref[...])
    up = _dot(xb, w3_ref[...])
    act = (gate * jax.nn.sigmoid(gate) * up).astype(BF16)
    acc_ref[...] += _dot(act, w2_ref[...])


def _ffn_kernel(xb_ref, w1_ref, w3_ref, w2_ref, x_ref, sh_ref, sc_ref, g_ref, o32_ref, o16_ref, *, alpha):
    f = pl.program_id(1)

    @pl.when(f == 0)
    def _():
        o32_ref[...] = jnp.zeros_like(o32_ref)

    _swiglu_step(xb_ref[...], w1_ref, w3_ref, w2_ref, o32_ref)

    @pl.when(f == pl.num_programs(1) - 1)
    def _():
        out = _post_norm(x_ref[...], o32_ref[...], sh_ref[...], sc_ref[...], g_ref[...], alpha)
        o32_ref[...] = out
        o16_ref[...] = out.astype(BF16)


def ffn_postnorm(xb, x, w13, w2, sh, sc, g, *, seq, alpha, tm, tf):
    m, d = x.shape
    ffn = w2.shape[0]
    tm, tf = _tile(m, tm), _tile(ffn, tf)
    nf = ffn // tf
    assert seq % tm == 0
    row = pl.BlockSpec((tm, d), lambda i, f: (i, 0))
    return pl.pallas_call(
        functools.partial(_ffn_kernel, alpha=alpha), name="ffn_postnorm",
        grid=(m // tm, nf),
        in_specs=[row,
                  pl.BlockSpec((d, tf), lambda i, f: (0, f)),
                  pl.BlockSpec((d, tf), lambda i, f: (0, nf + f)),
                  pl.BlockSpec((tf, d), lambda i, f: (f, 0)),
                  row] + _mod_specs(tm, seq, d),
        out_specs=[row, row],
        out_shape=[jax.ShapeDtypeStruct((m, d), F32), jax.ShapeDtypeStruct((m, d), BF16)],
        compiler_params=_params(("parallel", "arbitrary")),
    )(xb, w13, w13, w2, x, sh, sc, g)


def _router_kernel(x_ref, rwt_ref, rb_ref, e_ref, w_ref):
    logits = lax.dot_general(rwt_ref[...], x_ref[...], (((1,), (1,)), ((), ())),
                             preferred_element_type=F32, precision=HIGHEST) + rb_ref[...]
    n_exp = logits.shape[0]
    eid = lax.broadcasted_iota(jnp.int32, logits.shape, 0)
    m1 = jnp.max(logits, axis=0, keepdims=True)
    e1 = jnp.min(jnp.where(logits == m1, eid, n_exp), axis=0, keepdims=True)
    rest = jnp.where(eid == e1, -jnp.inf, logits)
    m2 = jnp.max(rest, axis=0, keepdims=True)
    e2 = jnp.min(jnp.where(rest == m2, eid, n_exp), axis=0, keepdims=True)
    ex = jnp.exp(m2 - m1)
    e_ref[0:1, :] = e1
    e_ref[1:2, :] = e2
    w_ref[0:1, :] = 1.0 / (1.0 + ex)
    w_ref[1:2, :] = ex / (1.0 + ex)


def router_top2(x, router_w, router_b, *, tm):
    m, d = x.shape
    n_exp = router_w.shape[1]
    tm = _tile(m, tm)
    out = pl.BlockSpec((TOP_K, tm), lambda i: (0, i))
    return pl.pallas_call(
        _router_kernel, name="router_top2",
        grid=(m // tm,),
        in_specs=[pl.BlockSpec((tm, d), lambda i: (i, 0)),
                  pl.BlockSpec((n_exp, d), lambda i: (0, 0)),
                  pl.BlockSpec((n_exp, 1), lambda i: (0, 0))],
        out_specs=[out, out],
        out_shape=[jax.ShapeDtypeStruct((TOP_K, m), jnp.int32), jax.ShapeDtypeStruct((TOP_K, m), F32)],
        compiler_params=_params(("parallel",)),
    )(x, router_w.T, router_b.reshape(n_exp, 1))


def _rank_kernel(e_ref, rank_ref, count_ref, carry_ref, *, n_exp):
    @pl.when(pl.program_id(0) == 0)
    def _():
        carry_ref[...] = jnp.zeros_like(carry_ref)

    e = e_ref[...]
    tb = e.shape[1]
    eid = lax.broadcasted_iota(jnp.int32, (n_exp, tb), 0)
    hit = (eid == e[0:1, :]) | (eid == e[1:2, :])
    onehot = hit.astype(F32)
    upper = (lax.broadcasted_iota(jnp.int32, (tb, tb), 0) <= lax.broadcasted_iota(jnp.int32, (tb, tb), 1))
    incl = _dot(hit.astype(BF16), upper.astype(BF16))
    excl = incl - onehot + carry_ref[...]
    r0 = jnp.sum(jnp.where(eid == e[0:1, :], excl, 0.0), axis=0, keepdims=True)
    r1 = jnp.sum(jnp.where(eid == e[1:2, :], excl, 0.0), axis=0, keepdims=True)
    rank_ref[0:1, :] = r0.astype(jnp.int32)
    rank_ref[1:2, :] = r1.astype(jnp.int32)
    carry_ref[...] += jnp.sum(onehot, axis=1, keepdims=True)
    count_ref[...] = carry_ref[...].astype(jnp.int32)


def expert_ranks(e_idx, n_exp, *, tb):
    m = e_idx.shape[1]
    tb = _tile(m, tb)
    blk = pl.BlockSpec((TOP_K, tb), lambda i: (0, i))
    return pl.pallas_call(
        functools.partial(_rank_kernel, n_exp=n_exp), name="expert_ranks",
        grid=(m // tb,),
        in_specs=[blk],
        out_specs=[blk, pl.BlockSpec((n_exp, 1), lambda i: (0, 0))],
        out_shape=[jax.ShapeDtypeStruct((TOP_K, m), jnp.int32), jax.ShapeDtypeStruct((n_exp, 1), jnp.int32)],
        scratch_shapes=[pltpu.VMEM((n_exp, 1), F32)],
        compiler_params=_params(("arbitrary",)),
    )(e_idx)


def _dispatch_kernel(pad_ref, dest_ref, x_ref, xs_ref, zero_ref, sem, zsem):
    tm, d = x_ref.shape
    n_pad = pad_ref.shape[1]

    @pl.when(pl.program_id(0) == 0)
    def _():
        zero_ref[...] = jnp.zeros_like(zero_ref)

        def zero_copy(r):
            return pltpu.make_async_copy(zero_ref, xs_ref.at[pl.ds(r, 1), :], zsem)

        for e in range(n_pad):
            lax.fori_loop(pad_ref[0, e], pad_ref[1, e], lambda r, c: (zero_copy(r).start(), c)[1], 0)
        for e in range(n_pad):
            lax.fori_loop(pad_ref[0, e], pad_ref[1, e], lambda r, c: (zero_copy(r).wait(), c)[1], 0)

    def issue(r, carry):
        for k in range(TOP_K):
            pltpu.make_async_copy(x_ref.at[pl.ds(r, 1), :], xs_ref.at[pl.ds(dest_ref[0, k, r], 1), :], sem).start()
        return carry

    lax.fori_loop(0, tm, issue, 0, unroll=8)
    for k in range(TOP_K):
        pltpu.make_async_copy(x_ref, xs_ref.at[pl.ds(0, tm), :], sem).wait()


def dispatch_rows(x, dest, pad_bounds, n_rows, *, tm):
    m, d = x.shape
    tm = _tile(m, tm)
    dest_t = dest.reshape(TOP_K, m // tm, tm).transpose(1, 0, 2)
    return pl.pallas_call(
        _dispatch_kernel, name="dispatch_rows",
        grid=(m // tm,),
        in_specs=[pl.BlockSpec(memory_space=pltpu.SMEM),
                  pl.BlockSpec((1, TOP_K, tm), lambda i: (i, 0, 0), memory_space=pltpu.SMEM),
                  pl.BlockSpec((tm, d), lambda i: (i, 0))],
        out_specs=pl.BlockSpec(memory_space=pl.ANY),
        out_shape=jax.ShapeDtypeStruct((n_rows, d), x.dtype),
        scratch_shapes=[pltpu.VMEM((1, d), x.dtype), pltpu.SemaphoreType.DMA(()), pltpu.SemaphoreType.DMA(())],
        compiler_params=_params(("arbitrary",)),
    )(pad_bounds, dest_t, x)


def _expert_kernel(be_ref, nv_ref, rb_ref, xs_ref, w1_ref, w3_ref, w2_ref, ys_ref, xb_ref):
    del be_ref, rb_ref
    b, f = pl.program_id(0), pl.program_id(1)
    used = nv_ref[b] > 0

    @pl.when(f == 0)
    def _():
        ys_ref[...] = jnp.zeros_like(ys_ref)

    @pl.when(used & (f == 0))
    def _():
        xb_ref[...] = xs_ref[...].astype(BF16)

    @pl.when(used)
    def _():
        _swiglu_step(xb_ref[...], w1_ref, w3_ref, w2_ref, ys_ref)


def expert_blocks(xs, w13, w2, block_e, block_nv, block_row, *, rows, tf):
    n_rows, d = xs.shape
    n_exp, ffn, _ = w2.shape
    tf = _tile(ffn, tf)
    nf = ffn // tf
    nb = n_rows // rows
    fi = lambda b, f, nv: jnp.where(nv[b] > 0, f, nf - 1)
    grid_spec = pltpu.PrefetchScalarGridSpec(
        num_scalar_prefetch=3,
        grid=(nb, nf),
        in_specs=[pl.BlockSpec((rows, d), lambda b, f, be, nv, rb: (rb[b], 0)),
                  pl.BlockSpec((None, d, tf), lambda b, f, be, nv, rb: (be[b], 0, fi(b, f, nv))),
                  pl.BlockSpec((None, d, tf), lambda b, f, be, nv, rb: (be[b], 0, nf + fi(b, f, nv))),
                  pl.BlockSpec((None, tf, d), lambda b, f, be, nv, rb: (be[b], fi(b, f, nv), 0))],
        out_specs=pl.BlockSpec((rows, d), lambda b, f, be, nv, rb: (b, 0)),
        scratch_shapes=[pltpu.VMEM((rows, d), BF16)])
    return pl.pallas_call(
        _expert_kernel, name="expert_blocks",
        grid_spec=grid_spec,
        out_shape=jax.ShapeDtypeStruct((n_rows, d), F32),
        compiler_params=_params(("arbitrary", "arbitrary")),
    )(block_e, block_nv, block_row, xs, w13, w13, w2)


def _combine_kernel(dest_ref, ys_ref, w0_ref, w1_ref, x_ref, sh_ref, sc_ref, g_ref, o_ref, buf_ref, sem, *, alpha):
    tm = x_ref.shape[0]

    def issue(r, carry):
        for k in range(TOP_K):
            pltpu.make_async_copy(ys_ref.at[pl.ds(dest_ref[0, k, r], 1), :], buf_ref.at[k, pl.ds(r, 1), :], sem).start()
        return carry

    lax.fori_loop(0, tm, issue, 0, unroll=8)
    for k in range(TOP_K):
        pltpu.make_async_copy(ys_ref.at[pl.ds(0, tm), :], buf_ref.at[k], sem).wait()
    y = w0_ref[...] * buf_ref[0] + w1_ref[...] * buf_ref[1]
    o_ref[...] = _post_norm(x_ref[...], y, sh_ref[...], sc_ref[...], g_ref[...], alpha)


def combine_postnorm(ys, dest, top_w, x, sh, sc, g, *, seq, alpha, tm):
    m, d = x.shape
    tm = _tile(m, tm)
    assert seq % tm == 0
    dest_t = dest.reshape(TOP_K, m // tm, tm).transpose(1, 0, 2)
    row = pl.BlockSpec((tm, d), lambda i: (i, 0))
    col = pl.BlockSpec((tm, 1), lambda i: (i, 0))
    return pl.pallas_call(
        functools.partial(_combine_kernel, alpha=alpha), name="combine_postnorm",
        grid=(m // tm,),
        in_specs=[pl.BlockSpec((1, TOP_K, tm), lambda i: (i, 0, 0), memory_space=pltpu.SMEM),
                  pl.BlockSpec(memory_space=pl.ANY), col, col, row] + _mod_specs(tm, seq, d),
        out_specs=row,
        out_shape=jax.ShapeDtypeStruct((m, d), F32),
        scratch_shapes=[pltpu.VMEM((TOP_K, tm, d), F32), pltpu.SemaphoreType.DMA(())],
        compiler_params=_params(("arbitrary",)),
    )(dest_t, ys, top_w[0].reshape(m, 1), top_w[1].reshape(m, 1), x, sh, sc, g)


def moe_postnorm(x, router_w, router_b, w13, w2, sh, sc, g, *, seq, alpha):
    m, d = x.shape
    n_exp = router_w.shape[1]
    e_idx, top_w = router_top2(x, router_w, router_b, tm=1024)
    rank, counts = expert_ranks(e_idx, n_exp, tb=512)
    counts = counts[:, 0]
    padded = ((counts + MOE_ROWS - 1) // MOE_ROWS) * MOE_ROWS
    ends = jnp.cumsum(padded)
    starts = ends - padded
    expert_id = jnp.arange(n_exp, dtype=jnp.int32)[:, None, None]
    dest = rank + jnp.sum(jnp.where(e_idx[None] == expert_id, starts[:, None, None], 0), axis=0)
    dest = dest.astype(jnp.int32)
    n_rows = ((TOP_K * m + MOE_ROWS - 1) // MOE_ROWS) * MOE_ROWS + n_exp * MOE_ROWS
    pad_bounds = jnp.stack([jnp.append(starts + counts, ends[-1]), jnp.append(ends, n_rows)]).astype(jnp.int32)
    nb = n_rows // MOE_ROWS
    block_start = jnp.arange(nb, dtype=jnp.int32) * MOE_ROWS
    last_used = jnp.maximum(ends[-1] // MOE_ROWS - 1, 0)
    block_e = jnp.minimum(jnp.searchsorted(ends, block_start, side='right'), n_exp - 1).astype(jnp.int32)
    block_e = jnp.where(block_start < ends[-1], block_e, block_e[last_used])
    block_nv = jnp.clip(counts[block_e] - (block_start - starts[block_e]), 0, MOE_ROWS)
    block_nv = jnp.where(block_start < ends[-1], block_nv, 0).astype(jnp.int32)
    block_row = jnp.minimum(jnp.arange(nb, dtype=jnp.int32), last_used).astype(jnp.int32)
    xs = dispatch_rows(x, dest, pad_bounds, n_rows, tm=256)
    ys = expert_blocks(xs, w13, w2, block_e, block_nv, block_row, rows=MOE_ROWS, tf=1024)
    return combine_postnorm(ys, dest, top_w, x, sh, sc, g, seq=seq, alpha=alpha, tm=256)


def kernel(x, c, w_in, ssm_a_re, ssm_a_im, ssm_log_dt, ssm_b_re, ssm_b_im, ssm_c_re, ssm_c_im, ssm_d, glu_w, glu_b, sg_ln_g, sg_ln_b, sg_w, sg_b, conv_w, w_branch_a, w_branch_b, w_branch_c, w_o, ada_w, ada_b, ffn_w13, ffn_w2, moe_router_w, moe_router_b, moe_w13, moe_w2):
    bsz, seq, d = x.shape
    depth = w_in.shape[0]
    m = bsz * seq
    d_ssm = glu_w.shape[1]
    d_sg = sg_ln_g.shape[1]
    alpha = (2.0 * depth) ** 0.25
    assert conv_w.shape[-1] == d_sg and w_in.shape[2] == d_ssm + 5 * d_sg + 3 * d

    mod = ada_modulation(c, ada_w, ada_b)
    x32 = x.reshape(m, d)
    xb = x32.astype(BF16)
    for l in range(depth):
        sh_m, sc_m, g_m, sh_f, sc_f, g_f = [mod[l, :, k * d:(k + 1) * d].reshape(bsz, 1, d) for k in range(6)]
        w_l = w_in[l].astype(BF16)
        u_ssm = matmul(xb, w_l[:, :d_ssm], tm=1024, tn=512, name="inproj_ssm")
        p5 = matmul(xb, w_l[:, d_ssm:d_ssm + 5 * d_sg], tm=1024, tn=d_sg, name="inproj_gmlp_conv")
        gates = matmul(xb, w_l[:, d_ssm + 5 * d_sg:], tm=1024, tn=1024, sigmoid=True, name="inproj_gates")
        mats = ssm_matrices(ssm_a_re[l], ssm_a_im[l], ssm_log_dt[l], ssm_b_re[l], ssm_b_im[l],
                            ssm_c_re[l], ssm_c_im[l], ssm_d[l], SSM_CHUNK)
        y_ssm = ssm_scan(u_ssm, mats, bsz, seq, SSM_CHUNK)
        merged = mixer_tail(y_ssm, p5, gates, glu_w[l], glu_b[l], sg_ln_g[l], sg_ln_b[l], sg_w[l], sg_b[l],
                            conv_w[l], w_branch_a[l], w_branch_b[l], w_branch_c[l], seq=seq, tm=256)
        x32, xb = outproj_postnorm(merged, w_o[l].astype(BF16), x32, sh_m, sc_m, g_m, seq=seq, alpha=alpha, tm=512)
        if l % 2 == 0:
            x32, xb = ffn_postnorm(xb, x32, ffn_w13[l // 2].astype(BF16), ffn_w2[l // 2].astype(BF16),
                                   sh_f, sc_f, g_f, seq=seq, alpha=alpha, tm=512, tf=512)
        else:
            x32 = moe_postnorm(x32, moe_router_w[l // 2], moe_router_b[l // 2],
                               moe_w13[l // 2].astype(BF16), moe_w2[l // 2].astype(BF16),
                               sh_f, sc_f, g_f, seq=seq, alpha=alpha)
            xb = x32.astype(BF16)
    return x32.reshape(bsz, seq, d)
```

```python
import functools
import math

import jax
import jax.numpy as jnp
from jax import lax
from jax.experimental import pallas as pl
from jax.experimental.pallas import tpu as pltpu

BF16 = jnp.bfloat16
F32 = jnp.float32

LN_EPS = 1e-5
TOP_K = 2
SSM_CHUNK = 32
MOE_ROWS = 512
V7X_VMEM_LIMIT = 56 * 1024 * 1024
HIGHEST = lax.Precision.HIGHEST


def _tile(dim, pref):
    if dim <= pref:
        return dim
    for t in range(pref - pref % 128, 0, -128):
        if dim % t == 0:
            return t
    raise ValueError((dim, pref))


def _params(sem):
    return pltpu.CompilerParams(dimension_semantics=sem, vmem_limit_bytes=V7X_VMEM_LIMIT)


def _dot(a, b):
    return jnp.dot(a, b, preferred_element_type=F32)


def _ada_kernel(c_ref, w_ref, b_ref, o_ref):
    o_ref[...] = jnp.dot(c_ref[...], w_ref[...], preferred_element_type=F32, precision=HIGHEST) + b_ref[...]


def ada_modulation(c, ada_w, ada_b):
    nl, d, d6 = ada_w.shape
    bsz = c.shape[0]
    tn = _tile(d6, 1024)
    return pl.pallas_call(
        _ada_kernel, name="ada_modulation",
        grid=(nl, d6 // tn),
        in_specs=[pl.BlockSpec((bsz, d), lambda l, j: (0, 0)),
                  pl.BlockSpec((None, d, tn), lambda l, j: (l, 0, j)),
                  pl.BlockSpec((None, 1, tn), lambda l, j: (l, 0, j))],
        out_specs=pl.BlockSpec((None, bsz, tn), lambda l, j: (l, 0, j)),
        out_shape=jax.ShapeDtypeStruct((nl, bsz, d6), F32),
        compiler_params=_params(("parallel", "parallel")),
    )(c, ada_w, ada_b.reshape(nl, 1, d6))


def _mm_kernel(a_ref, b_ref, o_ref, *, sigmoid):
    acc = _dot(a_ref[...], b_ref[...])
    if sigmoid:
        acc = jax.nn.sigmoid(acc)
    o_ref[...] = acc.astype(o_ref.dtype)


def matmul(a, b, *, tm, tn, name, out_dtype=BF16, sigmoid=False):
    m, k = a.shape
    n = b.shape[1]
    tm, tn = _tile(m, tm), _tile(n, tn)
    return pl.pallas_call(
        functools.partial(_mm_kernel, sigmoid=sigmoid), name=name,
        grid=(m // tm, n // tn),
        in_specs=[pl.BlockSpec((tm, k), lambda i, j: (i, 0)),
                  pl.BlockSpec((k, tn), lambda i, j: (0, j))],
        out_specs=pl.BlockSpec((tm, tn), lambda i, j: (i, j)),
        out_shape=jax.ShapeDtypeStruct((m, n), out_dtype),
        compiler_params=_params(("parallel", "parallel")),
    )(a, b)


def ssm_matrices(a_re, a_im, log_dt, b_re, b_im, c_re, c_im, d_skip, t_chunk):
    g, n = a_re.shape
    c = b_re.shape[-1]
    dt = jnp.exp(log_dt)[:, None]
    abar_re = jnp.exp(dt * a_re) * jnp.cos(dt * a_im)
    abar_im = jnp.exp(dt * a_re) * jnp.sin(dt * a_im)
    den = jnp.square(a_re) + jnp.square(a_im)
    num_re = abar_re - 1.0
    f_re = (num_re * a_re + abar_im * a_im) / den
    f_im = (abar_im * a_re - num_re * a_im) / den
    bbar_re = f_re[..., None] * b_re - f_im[..., None] * b_im
    bbar_im = f_re[..., None] * b_im + f_im[..., None] * b_re
    k = jnp.arange(t_chunk + 1, dtype=a_re.dtype)[:, None, None]
    pw_mag = jnp.exp(k * (dt * a_re))
    pw_re = pw_mag * jnp.cos(k * (dt * a_im))
    pw_im = pw_mag * jnp.sin(k * (dt * a_im))
    e_re = pw_re[..., None] * bbar_re - pw_im[..., None] * bbar_im
    e_im = pw_re[..., None] * bbar_im + pw_im[..., None] * bbar_re
    kern = (jnp.einsum('gcn,kgnd->gkcd', c_re, e_re[:t_chunk], precision=HIGHEST)
            - jnp.einsum('gcn,kgnd->gkcd', c_im, e_im[:t_chunk], precision=HIGHEST))
    kern = kern.at[:, 0].add(d_skip[:, :, None] * jnp.eye(c, dtype=kern.dtype))
    step = jnp.arange(t_chunk)
    lag = step[:, None] - step[None, :]
    toe = kern[:, jnp.clip(lag, 0, t_chunk - 1)]
    toe = jnp.where((lag >= 0)[None, :, :, None, None], toe, 0.0)
    m_t = toe.transpose(0, 1, 3, 2, 4).reshape(g, t_chunk * c, t_chunk * c)
    p_re = e_re[:t_chunk][::-1].transpose(1, 2, 0, 3).reshape(g, n, t_chunk * c)
    p_im = e_im[:t_chunk][::-1].transpose(1, 2, 0, 3).reshape(g, n, t_chunk * c)
    p_t = jnp.concatenate([p_re, p_im], axis=1)
    q_re = (c_re[None] * pw_re[1:, :, None, :] - c_im[None] * pw_im[1:, :, None, :])
    q_im = -(c_re[None] * pw_im[1:, :, None, :] + c_im[None] * pw_re[1:, :, None, :])
    q_t = jnp.concatenate([q_re, q_im], axis=-1).transpose(1, 0, 2, 3).reshape(g, t_chunk * c, 2 * n)
    lam_re, lam_im = pw_re[t_chunk], pw_im[t_chunk]
    lam = jnp.stack([jnp.concatenate([lam_re, lam_re], axis=-1),
                     jnp.concatenate([-lam_im, lam_im], axis=-1)], axis=1)
    return m_t, p_t, q_t, lam


def _ssm_kernel(x_ref, mt_ref, pt_ref, qt_ref, lam_ref, o_ref, s_ref, ss_ref, h_ref, *, n_steps, bsz):
    tc = mt_ref.shape[0]
    xt = x_ref[...].reshape(tc, x_ref.shape[-1])
    s = _dot(pt_ref[...].astype(F32), xt.astype(F32)).T
    half = s.shape[1] // 2
    s_ref[...] = s
    ss_ref[...] = pltpu.roll(s, half, 1)
    a = lam_ref[0:1, :]
    bm = lam_ref[1:2, :]

    def step(j, carry):
        new = []
        for b in range(bsz):
            h, hs = carry[b]
            row = pl.ds(b * n_steps + j, 1)
            h_ref[row, :] = h
            new.append((a * h + bm * hs + s_ref[row, :], a * hs - bm * h + ss_ref[row, :]))
        return tuple(new)

    zero = jnp.zeros((1, 2 * half), F32)
    lax.fori_loop(0, n_steps, step, tuple((zero, zero) for _ in range(bsz)))
    y = _dot(mt_ref[...], xt)
    y = y + _dot(qt_ref[...], h_ref[...].T.astype(BF16))
    o_ref[...] = y.astype(o_ref.dtype).reshape(o_ref.shape)


def ssm_scan(u, mats, bsz, seq, t_chunk):
    m_t, p_t, q_t, lam = mats
    g, tc, _ = m_t.shape
    n2 = lam.shape[-1]
    c = tc // t_chunk
    n_steps = seq // t_chunk
    chunks = n_steps * bsz
    xt = u.reshape(chunks, t_chunk * g * c).T.reshape(t_chunk, g, c, chunks)
    xspec = pl.BlockSpec((t_chunk, None, c, chunks), lambda i: (0, i, 0, 0))
    wspec = lambda shape: pl.BlockSpec((None,) + shape, lambda i: (i, 0, 0))
    yt = pl.pallas_call(
        functools.partial(_ssm_kernel, n_steps=n_steps, bsz=bsz), name="ssm_scan",
        grid=(g,),
        in_specs=[xspec, wspec((tc, tc)), wspec((n2, tc)), wspec((tc, n2)), wspec((2, n2))],
        out_specs=xspec,
        out_shape=jax.ShapeDtypeStruct((t_chunk, g, c, chunks), BF16),
        scratch_shapes=[pltpu.VMEM((chunks, n2), F32) for _ in range(3)],
        compiler_params=_params(("parallel",)),
    )(xt, m_t.astype(BF16), p_t.astype(BF16), q_t.astype(BF16), lam)
    return yt.reshape(t_chunk * g * c, chunks).T.reshape(bsz * seq, g * c)


def _layer_norm(z):
    mu = jnp.mean(z, axis=-1, keepdims=True)
    zc = z - mu
    var = jnp.mean(zc * zc, axis=-1, keepdims=True)
    return zc * lax.rsqrt(var + LN_EPS)


def _mixer_kernel(ya_ref, u_ref, v_ref, b_ref, c_ref, h_ref, ch_ref, hh_ref, g0_ref, g1_ref, g2_ref,
                  gluw_ref, glub_ref, lng_ref, lnb_ref, sgw_ref, sgb_ref, cw_ref, wa_ref, wb_ref, wc_ref,
                  o_ref, z_ref, vs_ref, zb_ref, *, tm, seq, chunk, heads):
    z = jax.nn.gelu(ya_ref[...].astype(F32), approximate=True)
    za = z * jax.nn.sigmoid(_dot(z.astype(BF16), gluw_ref[...]) + glub_ref[...])
    merged = g0_ref[...].astype(F32) * _dot(za.astype(BF16), wa_ref[...])

    d_sg = v_ref.shape[-1]
    head_dim = d_sg // heads
    v = (_layer_norm(v_ref[...].astype(F32)) * lng_ref[...] + lnb_ref[...]).astype(BF16)
    col_head = lax.broadcasted_iota(jnp.int32, (chunk, d_sg), 1) // head_dim
    for ci in range(tm // chunk):
        vc = v[ci * chunk:(ci + 1) * chunk, :]
        for hd in range(heads):
            vs_ref[hd * chunk:(hd + 1) * chunk, :] = jnp.where(col_head == hd, vc, jnp.zeros_like(vc))
        s = _dot(sgw_ref[...], vs_ref[...]) + sgb_ref[...]
        zb_ref[ci * chunk:(ci + 1) * chunk, :] = (u_ref[ci * chunk:(ci + 1) * chunk, :].astype(F32) * s).astype(BF16)
    merged = merged + g1_ref[...].astype(F32) * _dot(zb_ref[...], wb_ref[...])

    first = (pl.program_id(0) * tm) % seq == 0
    halo = ch_ref[...].astype(F32) * hh_ref[...].astype(F32)
    z_ref[0:8, :] = jnp.where(first, jnp.zeros_like(halo), halo)
    z_ref[8:, :] = c_ref[...].astype(F32) * h_ref[...].astype(F32)
    conv = (cw_ref[0:1, :] * z_ref[pl.ds(6, tm), :] + cw_ref[1:2, :] * z_ref[pl.ds(7, tm), :]
            + cw_ref[2:3, :] * z_ref[pl.ds(8, tm), :])
    zc = b_ref[...].astype(F32) * conv
    merged = merged + g2_ref[...].astype(F32) * _dot(zc.astype(BF16), wc_ref[...])
    o_ref[...] = merged.astype(o_ref.dtype)


def mixer_tail(ya, p5, gates, glu_w, glu_b, ln_g, ln_b, sg_w, sg_b, conv_w, wa, wb, wc, *, seq, tm):
    m, d_ssm = ya.shape
    d_sg = p5.shape[1] // 5
    d_model = wa.shape[1]
    heads, chunk, _ = sg_w.shape
    tm = _tile(m, tm)
    assert tm % chunk == 0 and seq % tm == 0 and conv_w.shape[0] == 3
    w_low = jnp.where(jnp.tril(jnp.ones((chunk, chunk), dtype=bool))[None], sg_w, 0.0)
    sgw_cat = w_low.transpose(1, 0, 2).reshape(chunk, heads * chunk).astype(BF16)
    sgb_full = jnp.repeat(sg_b.T, d_sg // heads, axis=1)
    seg = lambda k: pl.BlockSpec((tm, d_sg), lambda i: (i, k))
    halo = lambda k: pl.BlockSpec((8, d_sg), lambda i: (jnp.maximum(i * (tm // 8) - 1, 0), k))
    gate = lambda k: pl.BlockSpec((tm, d_model), lambda i: (i, k))
    full = lambda a: pl.BlockSpec(a.shape, lambda i: (0,) * a.ndim)
    consts = [glu_w.astype(BF16), glu_b.reshape(1, -1), ln_g.reshape(1, -1), ln_b.reshape(1, -1),
              sgw_cat, sgb_full, conv_w.reshape(3, -1), wa.astype(BF16), wb.astype(BF16), wc.astype(BF16)]
    return pl.pallas_call(
        functools.partial(_mixer_kernel, tm=tm, seq=seq, chunk=chunk, heads=heads), name="mixer_tail",
        grid=(m // tm,),
        in_specs=[pl.BlockSpec((tm, d_ssm), lambda i: (i, 0)),
                  seg(0), seg(1), seg(2), seg(3), seg(4), halo(3), halo(4),
                  gate(0), gate(1), gate(2)] + [full(a) for a in consts],
        out_specs=pl.BlockSpec((tm, d_model), lambda i: (i, 0)),
        out_shape=jax.ShapeDtypeStruct((m, d_model), BF16),
        scratch_shapes=[pltpu.VMEM((tm + 8, d_sg), F32),
                        pltpu.VMEM((heads * chunk, d_sg), BF16),
                        pltpu.VMEM((tm, d_sg), BF16)],
        compiler_params=_params(("parallel",)),
    )(ya, p5, p5, p5, p5, p5, p5, p5, gates, gates, gates, *consts)


def _post_norm(x, y, sh, sc, g, alpha):
    z = alpha * x + (1.0 + g) * y
    return _layer_norm(z) * (1.0 + sc) + sh


def _mod_specs(tm, seq, d):
    spec = pl.BlockSpec((None, 1, d), lambda i, *_: ((i * tm) // seq, 0, 0))
    return [spec, spec, spec]


def _route_top2(x, rwt_ref, rb_ref, e_ref, w_ref):
    logits = lax.dot_general(rwt_ref[...], x, (((1,), (1,)), ((), ())),
                             preferred_element_type=F32, precision=HIGHEST) + rb_ref[...]
    n_exp = logits.shape[0]
    eid = lax.broadcasted_iota(jnp.int32, logits.shape, 0)
    m1 = jnp.max(logits, axis=0, keepdims=True)
    e1 = jnp.min(jnp.where(logits == m1, eid, n_exp), axis=0, keepdims=True)
    rest = jnp.where(eid == e1, -jnp.inf, logits)
    m2 = jnp.max(rest, axis=0, keepdims=True)
    e2 = jnp.min(jnp.where(rest == m2, eid, n_exp), axis=0, keepdims=True)
    ex = jnp.exp(m2 - m1)
    e_ref[0:1, :] = e1
    e_ref[1:2, :] = e2
    w_ref[0:1, :] = 1.0 / (1.0 + ex)
    w_ref[1:2, :] = ex / (1.0 + ex)


def _outproj_kernel(a_ref, w_ref, x_ref, sh_ref, sc_ref, g_ref, o32_ref, o16_ref, *, alpha):
    y = _dot(a_ref[...], w_ref[...])
    out = _post_norm(x_ref[...], y, sh_ref[...], sc_ref[...], g_ref[...], alpha)
    o32_ref[...] = out
    o16_ref[...] = out.astype(BF16)


def _outproj_route_kernel(a_ref, w_ref, x_ref, sh_ref, sc_ref, g_ref, rwt_ref, rb_ref, o32_ref, e_ref, tw_ref, *, alpha):
    y = _dot(a_ref[...], w_ref[...])
    out = _post_norm(x_ref[...], y, sh_ref[...], sc_ref[...], g_ref[...], alpha)
    o32_ref[...] = out
    _route_top2(out, rwt_ref, rb_ref, e_ref, tw_ref)


def outproj_postnorm(a, w_o, x, sh, sc, g, *, seq, alpha, tm, router=None):
    m, d = x.shape
    tm = _tile(m, tm)
    assert seq % tm == 0
    row = pl.BlockSpec((tm, d), lambda i: (i, 0))
    in_specs = [pl.BlockSpec((tm, a.shape[1]), lambda i: (i, 0)),
                pl.BlockSpec(w_o.shape, lambda i: (0, 0)), row] + _mod_specs(tm, seq, d)
    if router is None:
        return pl.pallas_call(
            functools.partial(_outproj_kernel, alpha=alpha), name="outproj_postnorm",
            grid=(m // tm,),
            in_specs=in_specs,
            out_specs=[row, row],
            out_shape=[jax.ShapeDtypeStruct((m, d), F32), jax.ShapeDtypeStruct((m, d), BF16)],
            compiler_params=_params(("parallel",)),
        )(a, w_o, x, sh, sc, g)
    router_w, router_b = router
    n_exp = router_w.shape[1]
    top = pl.BlockSpec((TOP_K, tm), lambda i: (0, i))
    return pl.pallas_call(
        functools.partial(_outproj_route_kernel, alpha=alpha), name="outproj_postnorm_route",
        grid=(m // tm,),
        in_specs=in_specs + [pl.BlockSpec((n_exp, d), lambda i: (0, 0)), pl.BlockSpec((n_exp, 1), lambda i: (0, 0))],
        out_specs=[row, top, top],
        out_shape=[jax.ShapeDtypeStruct((m, d), F32), jax.ShapeDtypeStruct((TOP_K, m), jnp.int32),
                   jax.ShapeDtypeStruct((TOP_K, m), F32)],
        compiler_params=_params(("parallel",)),
    )(a, w_o, x, sh, sc, g, router_w.T, router_b.reshape(n_exp, 1))


def _swiglu_step(xb, w1_ref, w3_ref, w2_ref, acc_ref):
    gate = _dot(xb, w1_ref[...])
    up = _dot(xb, w3_ref[...])
    act = (gate * jax.nn.sigmoid(gate) * up).astype(BF16)
    acc_ref[...] += _dot(act, w2_ref[...])


def _ffn_kernel(xb_ref, w1_ref, w3_ref, w2_ref, x_ref, sh_ref, sc_ref, g_ref, o32_ref, o16_ref, *, alpha):
    f = pl.program_id(1)

    @pl.when(f == 0)
    def _():
        o32_ref[...] = jnp.zeros_like(o32_ref)

    _swiglu_step(xb_ref[...], w1_ref, w3_ref, w2_ref, o32_ref)

    @pl.when(f == pl.num_programs(1) - 1)
    def _():
        out = _post_norm(x_ref[...], o32_ref[...], sh_ref[...], sc_ref[...], g_ref[...], alpha)
        o32_ref[...] = out
        o16_ref[...] = out.astype(BF16)


def ffn_postnorm(xb, x, w13, w2, sh, sc, g, *, seq, alpha, tm, tf):
    m, d = x.shape
    ffn = w2.shape[0]
    tm, tf = _tile(m, tm), _tile(ffn, tf)
    nf = ffn // tf
    assert seq % tm == 0
    row = pl.BlockSpec((tm, d), lambda i, f: (i, 0))
    return pl.pallas_call(
        functools.partial(_ffn_kernel, alpha=alpha), name="ffn_postnorm",
        grid=(m // tm, nf),
        in_specs=[row,
                  pl.BlockSpec((d, tf), lambda i, f: (0, f)),
                  pl.BlockSpec((d, tf), lambda i, f: (0, nf + f)),
                  pl.BlockSpec((tf, d), lambda i, f: (f, 0)),
                  row] + _mod_specs(tm, seq, d),
        out_specs=[row, row],
        out_shape=[jax.ShapeDtypeStruct((m, d), F32), jax.ShapeDtypeStruct((m, d), BF16)],
        compiler_params=_params(("parallel", "arbitrary")),
    )(xb, w13, w13, w2, x, sh, sc, g)


def _rank_kernel(e_ref, rank_ref, count_ref, carry_ref, *, n_exp):
    @pl.when(pl.program_id(0) == 0)
    def _():
        carry_ref[...] = jnp.zeros_like(carry_ref)

    e = e_ref[...]
    tb = e.shape[1]
    eid = lax.broadcasted_iota(jnp.int32, (n_exp, tb), 0)
    hit = (eid == e[0:1, :]) | (eid == e[1:2, :])
    onehot = hit.astype(F32)
    upper = (lax.broadcasted_iota(jnp.int32, (tb, tb), 0) <= lax.broadcasted_iota(jnp.int32, (tb, tb), 1))
    incl = _dot(hit.astype(BF16), upper.astype(BF16))
    excl = incl - onehot + carry_ref[...]
    r0 = jnp.sum(jnp.where(eid == e[0:1, :], excl, 0.0), axis=0, keepdims=True)
    r1 = jnp.sum(jnp.where(eid == e[1:2, :], excl, 0.0), axis=0, keepdims=True)
    rank_ref[0:1, :] = r0.astype(jnp.int32)
    rank_ref[1:2, :] = r1.astype(jnp.int32)
    carry_ref[...] += jnp.sum(onehot, axis=1, keepdims=True)
    count_ref[...] = carry_ref[...].astype(jnp.int32)


def expert_ranks(e_idx, n_exp, *, tb):
    m = e_idx.shape[1]
    tb = _tile(m, tb)
    blk = pl.BlockSpec((TOP_K, tb), lambda i: (0, i))
    return pl.pallas_call(
        functools.partial(_rank_kernel, n_exp=n_exp), name="expert_ranks",
        grid=(m // tb,),
        in_specs=[blk],
        out_specs=[blk, pl.BlockSpec((n_exp, 1), lambda i: (0, 0))],
        out_shape=[jax.ShapeDtypeStruct((TOP_K, m), jnp.int32), jax.ShapeDtypeStruct((n_exp, 1), jnp.int32)],
        scratch_shapes=[pltpu.VMEM((n_exp, 1), F32)],
        compiler_params=_params(("arbitrary",)),
    )(e_idx)


def _dispatch_kernel(pad_ref, dest_ref, x_ref, xs_ref, zero_ref, sem, zsem):
    tm, d = x_ref.shape
    n_pad = pad_ref.shape[1]

    @pl.when(pl.program_id(0) == 0)
    def _():
        zero_ref[...] = jnp.zeros_like(zero_ref)

        def zero_copy(r):
            return pltpu.make_async_copy(zero_ref, xs_ref.at[pl.ds(r, 1), :], zsem)

        for e in range(n_pad):
            lax.fori_loop(pad_ref[0, e], pad_ref[1, e], lambda r, c: (zero_copy(r).start(), c)[1], 0)
        for e in range(n_pad):
            lax.fori_loop(pad_ref[0, e], pad_ref[1, e], lambda r, c: (zero_copy(r).wait(), c)[1], 0)

    def issue(r, carry):
        for k in range(TOP_K):
            pltpu.make_async_copy(x_ref.at[pl.ds(r, 1), :], xs_ref.at[pl.ds(dest_ref[0, k, r], 1), :], sem).start()
        return carry

    lax.fori_loop(0, tm, issue, 0, unroll=8)
    for k in range(TOP_K):
        pltpu.make_async_copy(x_ref, xs_ref.at[pl.ds(0, tm), :], sem).wait()


def dispatch_rows(x, dest, pad_bounds, n_rows, *, tm):
    m, d = x.shape
    tm = _tile(m, tm)
    dest_t = dest.reshape(TOP_K, m // tm, tm).transpose(1, 0, 2)
    return pl.pallas_call(
        _dispatch_kernel, name="dispatch_rows",
        grid=(m // tm,),
        in_specs=[pl.BlockSpec(memory_space=pltpu.SMEM),
                  pl.BlockSpec((1, TOP_K, tm), lambda i: (i, 0, 0), memory_space=pltpu.SMEM),
                  pl.BlockSpec((tm, d), lambda i: (i, 0))],
        out_specs=pl.BlockSpec(memory_space=pl.ANY),
        out_shape=jax.ShapeDtypeStruct((n_rows, d), x.dtype),
        scratch_shapes=[pltpu.VMEM((1, d), x.dtype), pltpu.SemaphoreType.DMA(()), pltpu.SemaphoreType.DMA(())],
        compiler_params=_params(("arbitrary",)),
    )(pad_bounds, dest_t, x)


def _expert_kernel(be_ref, nv_ref, rb_ref, xs_ref, w1_ref, w3_ref, w2_ref, ys_ref, xb_ref):
    del be_ref, rb_ref
    b, f = pl.program_id(0), pl.program_id(1)
    used = nv_ref[b] > 0

    @pl.when(f == 0)
    def _():
        ys_ref[...] = jnp.zeros_like(ys_ref)

    @pl.when(used & (f == 0))
    def _():
        xb_ref[...] = xs_ref[...].astype(BF16)

    @pl.when(used)
    def _():
        _swiglu_step(xb_ref[...], w1_ref, w3_ref, w2_ref, ys_ref)


def expert_blocks(xs, w13, w2, block_e, block_nv, block_row, *, rows, tf):
    n_rows, d = xs.shape
    n_exp, ffn, _ = w2.shape
    tf = _tile(ffn, tf)
    nf = ffn // tf
    nb = n_rows // rows
    fi = lambda b, f, nv: jnp.where(nv[b] > 0, f, nf - 1)
    grid_spec = pltpu.PrefetchScalarGridSpec(
        num_scalar_prefetch=3,
        grid=(nb, nf),
        in_specs=[pl.BlockSpec((rows, d), lambda b, f, be, nv, rb: (rb[b], 0)),
                  pl.BlockSpec((None, d, tf), lambda b, f, be, nv, rb: (be[b], 0, fi(b, f, nv))),
                  pl.BlockSpec((None, d, tf), lambda b, f, be, nv, rb: (be[b], 0, nf + fi(b, f, nv))),
                  pl.BlockSpec((None, tf, d), lambda b, f, be, nv, rb: (be[b], fi(b, f, nv), 0))],
        out_specs=pl.BlockSpec((rows, d), lambda b, f, be, nv, rb: (b, 0)),
        scratch_shapes=[pltpu.VMEM((rows, d), BF16)])
    return pl.pallas_call(
        _expert_kernel, name="expert_blocks",
        grid_spec=grid_spec,
        out_shape=jax.ShapeDtypeStruct((n_rows, d), F32),
        compiler_params=_params(("arbitrary", "arbitrary")),
    )(block_e, block_nv, block_row, xs, w13, w13, w2)


def _combine_kernel(dest_ref, ys_ref, w0_ref, w1_ref, x_ref, sh_ref, sc_ref, g_ref, o_ref, buf_ref, sem, *, alpha):
    tm = x_ref.shape[0]

    def issue(r, carry):
        for k in range(TOP_K):
            pltpu.make_async_copy(ys_ref.at[pl.ds(dest_ref[0, k, r], 1), :], buf_ref.at[k, pl.ds(r, 1), :], sem).start()
        return carry

    lax.fori_loop(0, tm, issue, 0, unroll=8)
    for k in range(TOP_K):
        pltpu.make_async_copy(ys_ref.at[pl.ds(0, tm), :], buf_ref.at[k], sem).wait()
    y = w0_ref[...] * buf_ref[0] + w1_ref[...] * buf_ref[1]
    o_ref[...] = _post_norm(x_ref[...], y, sh_ref[...], sc_ref[...], g_ref[...], alpha)


def combine_postnorm(ys, dest, top_w, x, sh, sc, g, *, seq, alpha, tm):
    m, d = x.shape
    tm = _tile(m, tm)
    assert seq % tm == 0
    dest_t = dest.reshape(TOP_K, m // tm, tm).transpose(1, 0, 2)
    row = pl.BlockSpec((tm, d), lambda i: (i, 0))
    col = pl.BlockSpec((tm, 1), lambda i: (i, 0))
    return pl.pallas_call(
        functools.partial(_combine_kernel, alpha=alpha), name="combine_postnorm",
        grid=(m // tm,),
        in_specs=[pl.BlockSpec((1, TOP_K, tm), lambda i: (i, 0, 0), memory_space=pltpu.SMEM),
                  pl.BlockSpec(memory_space=pl.ANY), col, col, row] + _mod_specs(tm, seq, d),
        out_specs=row,
        out_shape=jax.ShapeDtypeStruct((m, d), F32),
        scratch_shapes=[pltpu.VMEM((TOP_K, tm, d), F32), pltpu.SemaphoreType.DMA(())],
        compiler_params=_params(("arbitrary",)),
    )(dest_t, ys, top_w[0].reshape(m, 1), top_w[1].reshape(m, 1), x, sh, sc, g)


def moe_postnorm(x, e_idx, top_w, w13, w2, sh, sc, g, *, seq, alpha):
    m, d = x.shape
    n_exp = w2.shape[0]
    rank, counts = expert_ranks(e_idx, n_exp, tb=512)
    counts = counts[:, 0]
    padded = ((counts + MOE_ROWS - 1) // MOE_ROWS) * MOE_ROWS
    ends = jnp.cumsum(padded)
    starts = ends - padded
    expert_id = jnp.arange(n_exp, dtype=jnp.int32)[:, None, None]
    dest = rank + jnp.sum(jnp.where(e_idx[None] == expert_id, starts[:, None, None], 0), axis=0)
    dest = dest.astype(jnp.int32)
    n_rows = ((TOP_K * m + MOE_ROWS - 1) // MOE_ROWS) * MOE_ROWS + n_exp * MOE_ROWS
    pad_bounds = jnp.stack([jnp.append(starts + counts, ends[-1]), jnp.append(ends, n_rows)]).astype(jnp.int32)
    nb = n_rows // MOE_ROWS
    block_start = jnp.arange(nb, dtype=jnp.int32) * MOE_ROWS
    last_used = jnp.maximum(ends[-1] // MOE_ROWS - 1, 0)
    block_e = jnp.minimum(jnp.searchsorted(ends, block_start, side='right'), n_exp - 1).astype(jnp.int32)
    block_e = jnp.where(block_start < ends[-1], block_e, block_e[last_used])
    block_nv = jnp.clip(counts[block_e] - (block_start - starts[block_e]), 0, MOE_ROWS)
    block_nv = jnp.where(block_start < ends[-1], block_nv, 0).astype(jnp.int32)
    block_row = jnp.minimum(jnp.arange(nb, dtype=jnp.int32), last_used).astype(jnp.int32)
    xs = dispatch_rows(x, dest, pad_bounds, n_rows, tm=256)
    ys = expert_blocks(xs, w13, w2, block_e, block_nv, block_row, rows=MOE_ROWS, tf=1024)
    return combine_postnorm(ys, dest, top_w, x, sh, sc, g, seq=seq, alpha=alpha, tm=256)


def kernel(x, c, w_in, ssm_a_re, ssm_a_im, ssm_log_dt, ssm_b_re, ssm_b_im, ssm_c_re, ssm_c_im, ssm_d, glu_w, glu_b, sg_ln_g, sg_ln_b, sg_w, sg_b, conv_w, w_branch_a, w_branch_b, w_branch_c, w_o, ada_w, ada_b, ffn_w13, ffn_w2, moe_router_w, moe_router_b, moe_w13, moe_w2):
    bsz, seq, d = x.shape
    depth = w_in.shape[0]
    m = bsz * seq
    d_ssm = glu_w.shape[1]
    d_sg = sg_ln_g.shape[1]
    alpha = (2.0 * depth) ** 0.25
    assert conv_w.shape[-1] == d_sg and w_in.shape[2] == d_ssm + 5 * d_sg + 3 * d

    mod = ada_modulation(c, ada_w, ada_b)
    x32 = x.reshape(m, d)
    xb = x32.astype(BF16)
    for l in range(depth):
        sh_m, sc_m, g_m, sh_f, sc_f, g_f = [mod[l, :, k * d:(k + 1) * d].reshape(bsz, 1, d) for k in range(6)]
        w_l = w_in[l].astype(BF16)
        u_ssm = matmul(xb, w_l[:, :d_ssm], tm=1024, tn=512, name="inproj_ssm")
        p5 = matmul(xb, w_l[:, d_ssm:d_ssm + 5 * d_sg], tm=1024, tn=1280, name="inproj_gmlp_conv")
        gates = matmul(xb, w_l[:, d_ssm + 5 * d_sg:], tm=1024, tn=2048, sigmoid=True, name="inproj_gates")
        mats = ssm_matrices(ssm_a_re[l], ssm_a_im[l], ssm_log_dt[l], ssm_b_re[l], ssm_b_im[l],
                            ssm_c_re[l], ssm_c_im[l], ssm_d[l], SSM_CHUNK)
        y_ssm = ssm_scan(u_ssm, mats, bsz, seq, SSM_CHUNK)
        merged = mixer_tail(y_ssm, p5, gates, glu_w[l], glu_b[l], sg_ln_g[l], sg_ln_b[l], sg_w[l], sg_b[l],
                            conv_w[l], w_branch_a[l], w_branch_b[l], w_branch_c[l], seq=seq, tm=256)
        if l % 2 == 0:
            x32, xb = outproj_postnorm(merged, w_o[l].astype(BF16), x32, sh_m, sc_m, g_m, seq=seq, alpha=alpha, tm=512)
            x32, xb = ffn_postnorm(xb, x32, ffn_w13[l // 2].astype(BF16), ffn_w2[l // 2].astype(BF16),
                                   sh_f, sc_f, g_f, seq=seq, alpha=alpha, tm=512, tf=512)
        else:
            x32, e_idx, top_w = outproj_postnorm(merged, w_o[l].astype(BF16), x32, sh_m, sc_m, g_m, seq=seq,
                                                 alpha=alpha, tm=512, router=(moe_router_w[l // 2], moe_router_b[l // 2]))
            x32 = moe_postnorm(x32, e_idx, top_w,
                               moe_w13[l // 2].astype(BF16), moe_w2[l // 2].astype(BF16),
                               sh_f, sc_f, g_f, seq=seq, alpha=alpha)
            xb = x32.astype(BF16)
    return x32.reshape(bsz, seq, d)
```

```python
import functools
import math

import jax
import jax.numpy as jnp
from jax import lax
from jax.experimental import pallas as pl
from jax.experimental.pallas import tpu as pltpu

BF16 = jnp.bfloat16
F32 = jnp.float32

LN_EPS = 1e-5
TOP_K = 2
SSM_CHUNK = 32
MOE_ROWS = 512
V7X_VMEM_LIMIT = 56 * 1024 * 1024
HIGHEST = lax.Precision.HIGHEST


def _tile(dim, pref):
    if dim <= pref:
        return dim
    for t in range(pref - pref % 128, 0, -128):
        if dim % t == 0:
            return t
    raise ValueError((dim, pref))


def _params(sem):
    return pltpu.CompilerParams(dimension_semantics=sem, vmem_limit_bytes=V7X_VMEM_LIMIT)


def _dot(a, b):
    return jnp.dot(a, b, preferred_element_type=F32)


def _ada_kernel(c_ref, w_ref, b_ref, o_ref):
    o_ref[...] = jnp.dot(c_ref[...], w_ref[...], preferred_element_type=F32, precision=HIGHEST) + b_ref[...]


def ada_modulation(c, ada_w, ada_b):
    nl, d, d6 = ada_w.shape
    bsz = c.shape[0]
    tn = _tile(d6, 1024)
    return pl.pallas_call(
        _ada_kernel, name="ada_modulation",
        grid=(nl, d6 // tn),
        in_specs=[pl.BlockSpec((bsz, d), lambda l, j: (0, 0)),
                  pl.BlockSpec((None, d, tn), lambda l, j: (l, 0, j)),
                  pl.BlockSpec((None, 1, tn), lambda l, j: (l, 0, j))],
        out_specs=pl.BlockSpec((None, bsz, tn), lambda l, j: (l, 0, j)),
        out_shape=jax.ShapeDtypeStruct((nl, bsz, d6), F32),
        compiler_params=_params(("parallel", "parallel")),
    )(c, ada_w, ada_b.reshape(nl, 1, d6))


def _mm_kernel(a_ref, b_ref, o_ref, *, sigmoid):
    acc = _dot(a_ref[...], b_ref[...])
    if sigmoid:
        acc = jax.nn.sigmoid(acc)
    o_ref[...] = acc.astype(o_ref.dtype)


def matmul(a, b, *, tm, tn, name, out_dtype=BF16, sigmoid=False):
    m, k = a.shape
    n = b.shape[1]
    tm, tn = _tile(m, tm), _tile(n, tn)
    return pl.pallas_call(
        functools.partial(_mm_kernel, sigmoid=sigmoid), name=name,
        grid=(m // tm, n // tn),
        in_specs=[pl.BlockSpec((tm, k), lambda i, j: (i, 0)),
                  pl.BlockSpec((k, tn), lambda i, j: (0, j))],
        out_specs=pl.BlockSpec((tm, tn), lambda i, j: (i, j)),
        out_shape=jax.ShapeDtypeStruct((m, n), out_dtype),
        compiler_params=_params(("parallel", "parallel")),
    )(a, b)


def ssm_matrices(a_re, a_im, log_dt, b_re, b_im, c_re, c_im, d_skip, t_chunk):
    g, n = a_re.shape
    c = b_re.shape[-1]
    dt = jnp.exp(log_dt)[:, None]
    abar_re = jnp.exp(dt * a_re) * jnp.cos(dt * a_im)
    abar_im = jnp.exp(dt * a_re) * jnp.sin(dt * a_im)
    den = jnp.square(a_re) + jnp.square(a_im)
    num_re = abar_re - 1.0
    f_re = (num_re * a_re + abar_im * a_im) / den
    f_im = (abar_im * a_re - num_re * a_im) / den
    bbar_re = f_re[..., None] * b_re - f_im[..., None] * b_im
    bbar_im = f_re[..., None] * b_im + f_im[..., None] * b_re
    k = jnp.arange(t_chunk + 1, dtype=a_re.dtype)[:, None, None]
    pw_mag = jnp.exp(k * (dt * a_re))
    pw_re = pw_mag * jnp.cos(k * (dt * a_im))
    pw_im = pw_mag * jnp.sin(k * (dt * a_im))
    e_re = pw_re[..., None] * bbar_re - pw_im[..., None] * bbar_im
    e_im = pw_re[..., None] * bbar_im + pw_im[..., None] * bbar_re
    kern = (jnp.einsum('gcn,kgnd->gkcd', c_re, e_re[:t_chunk], precision=HIGHEST)
            - jnp.einsum('gcn,kgnd->gkcd', c_im, e_im[:t_chunk], precision=HIGHEST))
    kern = kern.at[:, 0].add(d_skip[:, :, None] * jnp.eye(c, dtype=kern.dtype))
    k_rev = kern[:, ::-1].transpose(0, 2, 1, 3).reshape(g, c, t_chunk * c)
    p_re = e_re[:t_chunk][::-1].transpose(1, 2, 0, 3).reshape(g, n, t_chunk * c)
    p_im = e_im[:t_chunk][::-1].transpose(1, 2, 0, 3).reshape(g, n, t_chunk * c)
    p_t = jnp.concatenate([p_re, p_im], axis=1)
    q_re = (c_re[None] * pw_re[1:, :, None, :] - c_im[None] * pw_im[1:, :, None, :])
    q_im = -(c_re[None] * pw_im[1:, :, None, :] + c_im[None] * pw_re[1:, :, None, :])
    q_t = jnp.concatenate([q_re, q_im], axis=-1).transpose(1, 0, 2, 3).reshape(g, t_chunk * c, 2 * n)
    lam_re, lam_im = pw_re[t_chunk], pw_im[t_chunk]
    lam = jnp.stack([jnp.concatenate([lam_re, lam_re], axis=-1),
                     jnp.concatenate([-lam_im, lam_im], axis=-1)], axis=1)
    return k_rev, p_t, q_t, lam


def _ssm_kernel(x_ref, krev_ref, pt_ref, qt_ref, lam_ref, o_ref, mt_ref, s_ref, ss_ref, h_ref, *, n_steps, bsz):
    c, tc = krev_ref.shape
    k_rev = krev_ref[...]
    base = jnp.concatenate([k_rev, jnp.zeros_like(k_rev)], axis=1)
    for t in range(tc // c):
        shift = tc - (t + 1) * c
        mt_ref[t * c:(t + 1) * c, :] = base[:, shift:shift + tc].astype(BF16)
    xt = x_ref[...].reshape(tc, x_ref.shape[-1])
    s = _dot(pt_ref[...].astype(F32), xt.astype(F32)).T
    half = s.shape[1] // 2
    s_ref[...] = s
    ss_ref[...] = pltpu.roll(s, half, 1)
    a = lam_ref[0:1, :]
    bm = lam_ref[1:2, :]

    def step(j, carry):
        new = []
        for b in range(bsz):
            h, hs = carry[b]
            row = pl.ds(b * n_steps + j, 1)
            h_ref[row, :] = h
            new.append((a * h + bm * hs + s_ref[row, :], a * hs - bm * h + ss_ref[row, :]))
        return tuple(new)

    zero = jnp.zeros((1, 2 * half), F32)
    lax.fori_loop(0, n_steps, step, tuple((zero, zero) for _ in range(bsz)))
    y = _dot(mt_ref[...], xt)
    y = y + _dot(qt_ref[...], h_ref[...].T.astype(BF16))
    o_ref[...] = y.astype(o_ref.dtype).reshape(o_ref.shape)


def ssm_scan(u, mats, bsz, seq, t_chunk):
    k_rev, p_t, q_t, lam = mats
    g, c, tc = k_rev.shape
    n2 = lam.shape[-1]
    n_steps = seq // t_chunk
    chunks = n_steps * bsz
    xt = u.reshape(chunks, t_chunk * g * c).T.reshape(t_chunk, g, c, chunks)
    xspec = pl.BlockSpec((t_chunk, None, c, chunks), lambda i: (0, i, 0, 0))
    wspec = lambda shape: pl.BlockSpec((None,) + shape, lambda i: (i, 0, 0))
    yt = pl.pallas_call(
        functools.partial(_ssm_kernel, n_steps=n_steps, bsz=bsz), name="ssm_scan",
        grid=(g,),
        in_specs=[xspec, wspec((c, tc)), wspec((n2, tc)), wspec((tc, n2)), wspec((2, n2))],
        out_specs=xspec,
        out_shape=jax.ShapeDtypeStruct((t_chunk, g, c, chunks), BF16),
        scratch_shapes=[pltpu.VMEM((tc, tc), BF16)] + [pltpu.VMEM((chunks, n2), F32) for _ in range(3)],
        compiler_params=_params(("parallel",)),
    )(xt, k_rev, p_t.astype(BF16), q_t.astype(BF16), lam)
    return yt.reshape(t_chunk * g * c, chunks).T.reshape(bsz * seq, g * c)


def _layer_norm(z):
    mu = jnp.mean(z, axis=-1, keepdims=True)
    zc = z - mu
    var = jnp.mean(zc * zc, axis=-1, keepdims=True)
    return zc * lax.rsqrt(var + LN_EPS)


def _mixer_kernel(ya_ref, u_ref, v_ref, b_ref, c_ref, h_ref, ch_ref, hh_ref, g0_ref, g1_ref, g2_ref,
                  gluw_ref, glub_ref, lng_ref, lnb_ref, sgw_ref, sgb_ref, cw_ref, wa_ref, wb_ref, wc_ref,
                  o_ref, z_ref, vs_ref, zb_ref, *, tm, seq, chunk, heads):
    z = jax.nn.gelu(ya_ref[...].astype(F32), approximate=True)
    za = z * jax.nn.sigmoid(_dot(z.astype(BF16), gluw_ref[...]) + glub_ref[...])
    merged = g0_ref[...].astype(F32) * _dot(za.astype(BF16), wa_ref[...])

    d_sg = v_ref.shape[-1]
    head_dim = d_sg // heads
    v = (_layer_norm(v_ref[...].astype(F32)) * lng_ref[...] + lnb_ref[...]).astype(BF16)
    col_head = lax.broadcasted_iota(jnp.int32, (chunk, d_sg), 1) // head_dim
    for ci in range(tm // chunk):
        vc = v[ci * chunk:(ci + 1) * chunk, :]
        for hd in range(heads):
            vs_ref[hd * chunk:(hd + 1) * chunk, :] = jnp.where(col_head == hd, vc, jnp.zeros_like(vc))
        s = _dot(sgw_ref[...], vs_ref[...]) + sgb_ref[...]
        zb_ref[ci * chunk:(ci + 1) * chunk, :] = (u_ref[ci * chunk:(ci + 1) * chunk, :].astype(F32) * s).astype(BF16)
    merged = merged + g1_ref[...].astype(F32) * _dot(zb_ref[...], wb_ref[...])

    first = (pl.program_id(0) * tm) % seq == 0
    halo = ch_ref[...].astype(F32) * hh_ref[...].astype(F32)
    z_ref[0:8, :] = jnp.where(first, jnp.zeros_like(halo), halo)
    z_ref[8:, :] = c_ref[...].astype(F32) * h_ref[...].astype(F32)
    conv = (cw_ref[0:1, :] * z_ref[pl.ds(6, tm), :] + cw_ref[1:2, :] * z_ref[pl.ds(7, tm), :]
            + cw_ref[2:3, :] * z_ref[pl.ds(8, tm), :])
    zc = b_ref[...].astype(F32) * conv
    merged = merged + g2_ref[...].astype(F32) * _dot(zc.astype(BF16), wc_ref[...])
    o_ref[...] = merged.astype(o_ref.dtype)


def mixer_tail(ya, p5, gates, glu_w, glu_b, ln_g, ln_b, sg_w, sg_b, conv_w, wa, wb, wc, *, seq, tm):
    m, d_ssm = ya.shape
    d_sg = p5.shape[1] // 5
    d_model = wa.shape[1]
    heads, chunk, _ = sg_w.shape
    tm = _tile(m, tm)
    assert tm % chunk == 0 and seq % tm == 0 and conv_w.shape[0] == 3
    w_low = jnp.where(jnp.tril(jnp.ones((chunk, chunk), dtype=bool))[None], sg_w, 0.0)
    sgw_cat = w_low.transpose(1, 0, 2).reshape(chunk, heads * chunk).astype(BF16)
    sgb_full = jnp.repeat(sg_b.T, d_sg // heads, axis=1)
    seg = lambda k: pl.BlockSpec((tm, d_sg), lambda i: (i, k))
    halo = lambda k: pl.BlockSpec((8, d_sg), lambda i: (jnp.maximum(i * (tm // 8) - 1, 0), k))
    gate = lambda k: pl.BlockSpec((tm, d_model), lambda i: (i, k))
    full = lambda a: pl.BlockSpec(a.shape, lambda i: (0,) * a.ndim)
    consts = [glu_w.astype(BF16), glu_b.reshape(1, -1), ln_g.reshape(1, -1), ln_b.reshape(1, -1),
              sgw_cat, sgb_full, conv_w.reshape(3, -1), wa.astype(BF16), wb.astype(BF16), wc.astype(BF16)]
    return pl.pallas_call(
        functools.partial(_mixer_kernel, tm=tm, seq=seq, chunk=chunk, heads=heads), name="mixer_tail",
        grid=(m // tm,),
        in_specs=[pl.BlockSpec((tm, d_ssm), lambda i: (i, 0)),
                  seg(0), seg(1), seg(2), seg(3), seg(4), halo(3), halo(4),
                  gate(0), gate(1), gate(2)] + [full(a) for a in consts],
        out_specs=pl.BlockSpec((tm, d_model), lambda i: (i, 0)),
        out_shape=jax.ShapeDtypeStruct((m, d_model), BF16),
        scratch_shapes=[pltpu.VMEM((tm + 8, d_sg), F32),
                        pltpu.VMEM((heads * chunk, d_sg), BF16),
                        pltpu.VMEM((tm, d_sg), BF16)],
        compiler_params=_params(("parallel",)),
    )(ya, p5, p5, p5, p5, p5, p5, p5, gates, gates, gates, *consts)


def _post_norm(x, y, sh, sc, g, alpha):
    z = alpha * x + (1.0 + g) * y
    return _layer_norm(z) * (1.0 + sc) + sh


def _mod_specs(tm, seq, d):
    spec = pl.BlockSpec((None, 1, d), lambda i, *_: ((i * tm) // seq, 0, 0))
    return [spec, spec, spec]


def _route_top2(x, rwt_ref, rb_ref, e_ref, w_ref):
    logits = lax.dot_general(rwt_ref[...], x, (((1,), (1,)), ((), ())),
                             preferred_element_type=F32, precision=HIGHEST) + rb_ref[...]
    n_exp = logits.shape[0]
    eid = lax.broadcasted_iota(jnp.int32, logits.shape, 0)
    m1 = jnp.max(logits, axis=0, keepdims=True)
    e1 = jnp.min(jnp.where(logits == m1, eid, n_exp), axis=0, keepdims=True)
    rest = jnp.where(eid == e1, -jnp.inf, logits)
    m2 = jnp.max(rest, axis=0, keepdims=True)
    e2 = jnp.min(jnp.where(rest == m2, eid, n_exp), axis=0, keepdims=True)
    ex = jnp.exp(m2 - m1)
    e_ref[0:1, :] = e1
    e_ref[1:2, :] = e2
    w_ref[0:1, :] = 1.0 / (1.0 + ex)
    w_ref[1:2, :] = ex / (1.0 + ex)


def _outproj_kernel(a_ref, w_ref, x_ref, sh_ref, sc_ref, g_ref, o32_ref, o16_ref, *, alpha):
    y = _dot(a_ref[...], w_ref[...])
    out = _post_norm(x_ref[...], y, sh_ref[...], sc_ref[...], g_ref[...], alpha)
    o32_ref[...] = out
    o16_ref[...] = out.astype(BF16)


def _outproj_route_kernel(a_ref, w_ref, x_ref, sh_ref, sc_ref, g_ref, rwt_ref, rb_ref, o32_ref, e_ref, tw_ref, *, alpha):
    y = _dot(a_ref[...], w_ref[...])
    out = _post_norm(x_ref[...], y, sh_ref[...], sc_ref[...], g_ref[...], alpha)
    o32_ref[...] = out
    _route_top2(out, rwt_ref, rb_ref, e_ref, tw_ref)


def outproj_postnorm(a, w_o, x, sh, sc, g, *, seq, alpha, tm, router=None):
    m, d = x.shape
    tm = _tile(m, tm)
    assert seq % tm == 0
    row = pl.BlockSpec((tm, d), lambda i: (i, 0))
    in_specs = [pl.BlockSpec((tm, a.shape[1]), lambda i: (i, 0)),
                pl.BlockSpec(w_o.shape, lambda i: (0, 0)), row] + _mod_specs(tm, seq, d)
    if router is None:
        return pl.pallas_call(
            functools.partial(_outproj_kernel, alpha=alpha), name="outproj_postnorm",
            grid=(m // tm,),
            in_specs=in_specs,
            out_specs=[row, row],
            out_shape=[jax.ShapeDtypeStruct((m, d), F32), jax.ShapeDtypeStruct((m, d), BF16)],
            compiler_params=_params(("parallel",)),
        )(a, w_o, x, sh, sc, g)
    router_w, router_b = router
    n_exp = router_w.shape[1]
    top = pl.BlockSpec((TOP_K, tm), lambda i: (0, i))
    return pl.pallas_call(
        functools.partial(_outproj_route_kernel, alpha=alpha), name="outproj_postnorm_route",
        grid=(m // tm,),
        in_specs=in_specs + [pl.BlockSpec((n_exp, d), lambda i: (0, 0)), pl.BlockSpec((n_exp, 1), lambda i: (0, 0))],
        out_specs=[row, top, top],
        out_shape=[jax.ShapeDtypeStruct((m, d), F32), jax.ShapeDtypeStruct((TOP_K, m), jnp.int32),
                   jax.ShapeDtypeStruct((TOP_K, m), F32)],
        compiler_params=_params(("parallel",)),
    )(a, w_o, x, sh, sc, g, router_w.T, router_b.reshape(n_exp, 1))


def _swiglu_step(xb, w1_ref, w3_ref, w2_ref, acc_ref):
    gate = _dot(xb, w1_ref[...])
    up = _dot(xb, w3_ref[...])
    act = (gate * jax.nn.sigmoid(gate) * up).astype(BF16)
    acc_ref[...] += _dot(act, w2_ref[...])


def _ffn_kernel(xb_ref, w1_ref, w3_ref, w2_ref, x_ref, sh_ref, sc_ref, g_ref, o32_ref, o16_ref, *, alpha):
    f = pl.program_id(1)

    @pl.when(f == 0)
    def _():
        o32_ref[...] = jnp.zeros_like(o32_ref)

    _swiglu_step(xb_ref[...], w1_ref, w3_ref, w2_ref, o32_ref)

    @pl.when(f == pl.num_programs(1) - 1)
    def _():
        out = _post_norm(x_ref[...], o32_ref[...], sh_ref[...], sc_ref[...], g_ref[...], alpha)
        o32_ref[...] = out
        o16_ref[...] = out.astype(BF16)


def ffn_postnorm(xb, x, w13, w2, sh, sc, g, *, seq, alpha, tm, tf):
    m, d = x.shape
    ffn = w2.shape[0]
    tm, tf = _tile(m, tm), _tile(ffn, tf)
    nf = ffn // tf
    assert seq % tm == 0
    row = pl.BlockSpec((tm, d), lambda i, f: (i, 0))
    return pl.pallas_call(
        functools.partial(_ffn_kernel, alpha=alpha), name="ffn_postnorm",
        grid=(m // tm, nf),
        in_specs=[row,
                  pl.BlockSpec((d, tf), lambda i, f: (0, f)),
                  pl.BlockSpec((d, tf), lambda i, f: (0, nf + f)),
                  pl.BlockSpec((tf, d), lambda i, f: (f, 0)),
                  row] + _mod_specs(tm, seq, d),
        out_specs=[row, row],
        out_shape=[jax.ShapeDtypeStruct((m, d), F32), jax.ShapeDtypeStruct((m, d), BF16)],
        compiler_params=_params(("parallel", "arbitrary")),
    )(xb, w13, w13, w2, x, sh, sc, g)


def _rank_kernel(e_ref, rank_ref, count_ref, carry_ref, *, n_exp):
    @pl.when(pl.program_id(0) == 0)
    def _():
        carry_ref[...] = jnp.zeros_like(carry_ref)

    e = e_ref[...]
    tb = e.shape[1]
    eid = lax.broadcasted_iota(jnp.int32, (n_exp, tb), 0)
    hit = (eid == e[0:1, :]) | (eid == e[1:2, :])
    onehot = hit.astype(F32)
    upper = (lax.broadcasted_iota(jnp.int32, (tb, tb), 0) <= lax.broadcasted_iota(jnp.int32, (tb, tb), 1))
    incl = _dot(hit.astype(BF16), upper.astype(BF16))
    excl = incl - onehot + carry_ref[...]
    r0 = jnp.sum(jnp.where(eid == e[0:1, :], excl, 0.0), axis=0, keepdims=True)
    r1 = jnp.sum(jnp.where(eid == e[1:2, :], excl, 0.0), axis=0, keepdims=True)
    rank_ref[0:1, :] = r0.astype(jnp.int32)
    rank_ref[1:2, :] = r1.astype(jnp.int32)
    carry_ref[...] += jnp.sum(onehot, axis=1, keepdims=True)
    count_ref[...] = carry_ref[...].astype(jnp.int32)


def expert_ranks(e_idx, n_exp, *, tb):
    m = e_idx.shape[1]
    tb = _tile(m, tb)
    blk = pl.BlockSpec((TOP_K, tb), lambda i: (0, i))
    return pl.pallas_call(
        functools.partial(_rank_kernel, n_exp=n_exp), name="expert_ranks",
        grid=(m // tb,),
        in_specs=[blk],
        out_specs=[blk, pl.BlockSpec((n_exp, 1), lambda i: (0, 0))],
        out_shape=[jax.ShapeDtypeStruct((TOP_K, m), jnp.int32), jax.ShapeDtypeStruct((n_exp, 1), jnp.int32)],
        scratch_shapes=[pltpu.VMEM((n_exp, 1), F32)],
        compiler_params=_params(("arbitrary",)),
    )(e_idx)


def _dispatch_kernel(pad_ref, dest_ref, x_ref, xs_ref, zero_ref, sem, zsem):
    tm, d = x_ref.shape
    n_pad = pad_ref.shape[1]

    @pl.when(pl.program_id(0) == 0)
    def _():
        zero_ref[...] = jnp.zeros_like(zero_ref)

        def zero_copy(r):
            return pltpu.make_async_copy(zero_ref, xs_ref.at[pl.ds(r, 1), :], zsem)

        for e in range(n_pad):
            lax.fori_loop(pad_ref[0, e], pad_ref[1, e], lambda r, c: (zero_copy(r).start(), c)[1], 0)
        for e in range(n_pad):
            lax.fori_loop(pad_ref[0, e], pad_ref[1, e], lambda r, c: (zero_copy(r).wait(), c)[1], 0)

    def issue(r, carry):
        for k in range(TOP_K):
            pltpu.make_async_copy(x_ref.at[pl.ds(r, 1), :], xs_ref.at[pl.ds(dest_ref[0, k, r], 1), :], sem).start()
        return carry

    lax.fori_loop(0, tm, issue, 0, unroll=8)
    for k in range(TOP_K):
        pltpu.make_async_copy(x_ref, xs_ref.at[pl.ds(0, tm), :], sem).wait()


def dispatch_rows(x, dest, pad_bounds, n_rows, *, tm):
    m, d = x.shape
    tm = _tile(m, tm)
    dest_t = dest.reshape(TOP_K, m // tm, tm).transpose(1, 0, 2)
    return pl.pallas_call(
        _dispatch_kernel, name="dispatch_rows",
        grid=(m // tm,),
        in_specs=[pl.BlockSpec(memory_space=pltpu.SMEM),
                  pl.BlockSpec((1, TOP_K, tm), lambda i: (i, 0, 0), memory_space=pltpu.SMEM),
                  pl.BlockSpec((tm, d), lambda i: (i, 0))],
        out_specs=pl.BlockSpec(memory_space=pl.ANY),
        out_shape=jax.ShapeDtypeStruct((n_rows, d), x.dtype),
        scratch_shapes=[pltpu.VMEM((1, d), x.dtype), pltpu.SemaphoreType.DMA(()), pltpu.SemaphoreType.DMA(())],
        compiler_params=_params(("arbitrary",)),
    )(pad_bounds, dest_t, x)


def _expert_kernel(be_ref, nv_ref, rb_ref, xs_ref, w1_ref, w3_ref, w2_ref, ys_ref, xb_ref):
    del be_ref, rb_ref
    b, f = pl.program_id(0), pl.program_id(1)
    used = nv_ref[b] > 0

    @pl.when(f == 0)
    def _():
        ys_ref[...] = jnp.zeros_like(ys_ref)

    @pl.when(used & (f == 0))
    def _():
        xb_ref[...] = xs_ref[...].astype(BF16)

    @pl.when(used)
    def _():
        _swiglu_step(xb_ref[...], w1_ref, w3_ref, w2_ref, ys_ref)


def expert_blocks(xs, w13, w2, block_e, block_nv, block_row, *, rows, tf):
    n_rows, d = xs.shape
    n_exp, ffn, _ = w2.shape
    tf = _tile(ffn, tf)
    nf = ffn // tf
    nb = n_rows // rows
    fi = lambda b, f, nv: jnp.where(nv[b] > 0, f, nf - 1)
    grid_spec = pltpu.PrefetchScalarGridSpec(
        num_scalar_prefetch=3,
        grid=(nb, nf),
        in_specs=[pl.BlockSpec((rows, d), lambda b, f, be, nv, rb: (rb[b], 0)),
                  pl.BlockSpec((None, d, tf), lambda b, f, be, nv, rb: (be[b], 0, fi(b, f, nv))),
                  pl.BlockSpec((None, d, tf), lambda b, f, be, nv, rb: (be[b], 0, nf + fi(b, f, nv))),
                  pl.BlockSpec((None, tf, d), lambda b, f, be, nv, rb: (be[b], fi(b, f, nv), 0))],
        out_specs=pl.BlockSpec((rows, d), lambda b, f, be, nv, rb: (b, 0)),
        scratch_shapes=[pltpu.VMEM((rows, d), BF16)])
    return pl.pallas_call(
        _expert_kernel, name="expert_blocks",
        grid_spec=grid_spec,
        out_shape=jax.ShapeDtypeStruct((n_rows, d), F32),
        compiler_params=_params(("arbitrary", "arbitrary")),
    )(block_e, block_nv, block_row, xs, w13, w13, w2)


def _combine_kernel(dest_ref, ys_ref, w0_ref, w1_ref, x_ref, sh_ref, sc_ref, g_ref, o_ref, buf_ref, sem, *, alpha):
    tm = x_ref.shape[0]

    def issue(r, carry):
        for k in range(TOP_K):
            pltpu.make_async_copy(ys_ref.at[pl.ds(dest_ref[0, k, r], 1), :], buf_ref.at[k, pl.ds(r, 1), :], sem).start()
        return carry

    lax.fori_loop(0, tm, issue, 0, unroll=8)
    for k in range(TOP_K):
        pltpu.make_async_copy(ys_ref.at[pl.ds(0, tm), :], buf_ref.at[k], sem).wait()
    y = w0_ref[...] * buf_ref[0] + w1_ref[...] * buf_ref[1]
    o_ref[...] = _post_norm(x_ref[...], y, sh_ref[...], sc_ref[...], g_ref[...], alpha)


def combine_postnorm(ys, dest, top_w, x, sh, sc, g, *, seq, alpha, tm):
    m, d = x.shape
    tm = _tile(m, tm)
    assert seq % tm == 0
    dest_t = dest.reshape(TOP_K, m // tm, tm).transpose(1, 0, 2)
    row = pl.BlockSpec((tm, d), lambda i: (i, 0))
    col = pl.BlockSpec((tm, 1), lambda i: (i, 0))
    return pl.pallas_call(
        functools.partial(_combine_kernel, alpha=alpha), name="combine_postnorm",
        grid=(m // tm,),
        in_specs=[pl.BlockSpec((1, TOP_K, tm), lambda i: (i, 0, 0), memory_space=pltpu.SMEM),
                  pl.BlockSpec(memory_space=pl.ANY), col, col, row] + _mod_specs(tm, seq, d),
        out_specs=row,
        out_shape=jax.ShapeDtypeStruct((m, d), F32),
        scratch_shapes=[pltpu.VMEM((TOP_K, tm, d), F32), pltpu.SemaphoreType.DMA(())],
        compiler_params=_params(("arbitrary",)),
    )(dest_t, ys, top_w[0].reshape(m, 1), top_w[1].reshape(m, 1), x, sh, sc, g)


def moe_postnorm(x, e_idx, top_w, w13, w2, sh, sc, g, *, seq, alpha):
    m, d = x.shape
    n_exp = w2.shape[0]
    rank, counts = expert_ranks(e_idx, n_exp, tb=512)
    counts = counts[:, 0]
    padded = ((counts + MOE_ROWS - 1) // MOE_ROWS) * MOE_ROWS
    ends = jnp.cumsum(padded)
    starts = ends - padded
    expert_id = jnp.arange(n_exp, dtype=jnp.int32)[:, None, None]
    dest = rank + jnp.sum(jnp.where(e_idx[None] == expert_id, starts[:, None, None], 0), axis=0)
    dest = dest.astype(jnp.int32)
    n_rows = ((TOP_K * m + MOE_ROWS - 1) // MOE_ROWS) * MOE_ROWS + n_exp * MOE_ROWS
    pad_bounds = jnp.stack([jnp.append(starts + counts, ends[-1]), jnp.append(ends, n_rows)]).astype(jnp.int32)
    nb = n_rows // MOE_ROWS
    block_start = jnp.arange(nb, dtype=jnp.int32) * MOE_ROWS
    last_used = jnp.maximum(ends[-1] // MOE_ROWS - 1, 0)
    block_e = jnp.minimum(jnp.sum(block_start[:, None] >= ends[None, :], axis=1), n_exp - 1).astype(jnp.int32)
    block_e = jnp.where(block_start < ends[-1], block_e, block_e[last_used])
    block_nv = jnp.clip(counts[block_e] - (block_start - starts[block_e]), 0, MOE_ROWS)
    block_nv = jnp.where(block_start < ends[-1], block_nv, 0).astype(jnp.int32)
    block_row = jnp.minimum(jnp.arange(nb, dtype=jnp.int32), last_used).astype(jnp.int32)
    xs = dispatch_rows(x, dest, pad_bounds, n_rows, tm=512)
    ys = expert_blocks(xs, w13, w2, block_e, block_nv, block_row, rows=MOE_ROWS, tf=1024)
    return combine_postnorm(ys, dest, top_w, x, sh, sc, g, seq=seq, alpha=alpha, tm=512)


def kernel(x, c, w_in, ssm_a_re, ssm_a_im, ssm_log_dt, ssm_b_re, ssm_b_im, ssm_c_re, ssm_c_im, ssm_d, glu_w, glu_b, sg_ln_g, sg_ln_b, sg_w, sg_b, conv_w, w_branch_a, w_branch_b, w_branch_c, w_o, ada_w, ada_b, ffn_w13, ffn_w2, moe_router_w, moe_router_b, moe_w13, moe_w2):
    bsz, seq, d = x.shape
    depth = w_in.shape[0]
    m = bsz * seq
    d_ssm = glu_w.shape[1]
    d_sg = sg_ln_g.shape[1]
    alpha = (2.0 * depth) ** 0.25
    assert conv_w.shape[-1] == d_sg and w_in.shape[2] == d_ssm + 5 * d_sg + 3 * d

    mod = ada_modulation(c, ada_w, ada_b)
    x32 = x.reshape(m, d)
    xb = x32.astype(BF16)
    for l in range(depth):
        sh_m, sc_m, g_m, sh_f, sc_f, g_f = [mod[l, :, k * d:(k + 1) * d].reshape(bsz, 1, d) for k in range(6)]
        w_ssm, w_p5, w_gates = [w_in[l, :, a:b].astype(BF16) for a, b in
                                ((0, d_ssm), (d_ssm, d_ssm + 5 * d_sg), (d_ssm + 5 * d_sg, w_in.shape[2]))]
        u_ssm = matmul(xb, w_ssm, tm=1024, tn=512, name="inproj_ssm")
        p5 = matmul(xb, w_p5, tm=1024, tn=1280, name="inproj_gmlp_conv")
        gates = matmul(xb, w_gates, tm=1024, tn=2048, sigmoid=True, name="inproj_gates")
        mats = ssm_matrices(ssm_a_re[l], ssm_a_im[l], ssm_log_dt[l], ssm_b_re[l], ssm_b_im[l],
                            ssm_c_re[l], ssm_c_im[l], ssm_d[l], SSM_CHUNK)
        y_ssm = ssm_scan(u_ssm, mats, bsz, seq, SSM_CHUNK)
        merged = mixer_tail(y_ssm, p5, gates, glu_w[l], glu_b[l], sg_ln_g[l], sg_ln_b[l], sg_w[l], sg_b[l],
                            conv_w[l], w_branch_a[l], w_branch_b[l], w_branch_c[l], seq=seq, tm=256)
        if l % 2 == 0:
            x32, xb = outproj_postnorm(merged, w_o[l].astype(BF16), x32, sh_m, sc_m, g_m, seq=seq, alpha=alpha, tm=512)
            x32, xb = ffn_postnorm(xb, x32, ffn_w13[l // 2].astype(BF16), ffn_w2[l // 2].astype(BF16),
                                   sh_f, sc_f, g_f, seq=seq, alpha=alpha, tm=512, tf=512)
        else:
            x32, e_idx, top_w = outproj_postnorm(merged, w_o[l].astype(BF16), x32, sh_m, sc_m, g_m, seq=seq,
                                                 alpha=alpha, tm=512, router=(moe_router_w[l // 2], moe_router_b[l // 2]))
            x32 = moe_postnorm(x32, e_idx, top_w,
                               moe_w13[l // 2].astype(BF16), moe_w2[l // 2].astype(BF16),
                               sh_f, sc_f, g_f, seq=seq, alpha=alpha)
            xb = x32.astype(BF16)
    return x32.reshape(bsz, seq, d)
```

```python
import functools
import math

import jax
import jax.numpy as jnp
from jax import lax
from jax.experimental import pallas as pl
from jax.experimental.pallas import tpu as pltpu

BF16 = jnp.bfloat16
F32 = jnp.float32

LN_EPS = 1e-5
TOP_K = 2
SSM_CHUNK = 32
MOE_ROWS = 512
V7X_VMEM_LIMIT = 56 * 1024 * 1024
HIGHEST = lax.Precision.HIGHEST


def _tile(dim, pref):
    if dim <= pref:
        return dim
    for t in range(pref - pref % 128, 0, -128):
        if dim % t == 0:
            return t
    raise ValueError((dim, pref))


def _params(sem):
    return pltpu.CompilerParams(dimension_semantics=sem, vmem_limit_bytes=V7X_VMEM_LIMIT)


def _dot(a, b):
    return jnp.dot(a, b, preferred_element_type=F32)


def _ada_kernel(c_ref, w_ref, b_ref, o_ref):
    o_ref[...] = jnp.dot(c_ref[...], w_ref[...], preferred_element_type=F32, precision=HIGHEST) + b_ref[...]


def ada_modulation(c, ada_w, ada_b):
    nl, d, d6 = ada_w.shape
    bsz = c.shape[0]
    tn = _tile(d6, 1024)
    return pl.pallas_call(
        _ada_kernel, name="ada_modulation",
        grid=(nl, d6 // tn),
        in_specs=[pl.BlockSpec((bsz, d), lambda l, j: (0, 0)),
                  pl.BlockSpec((None, d, tn), lambda l, j: (l, 0, j)),
                  pl.BlockSpec((None, 1, tn), lambda l, j: (l, 0, j))],
        out_specs=pl.BlockSpec((None, bsz, tn), lambda l, j: (l, 0, j)),
        out_shape=jax.ShapeDtypeStruct((nl, bsz, d6), F32),
        compiler_params=_params(("parallel", "parallel")),
    )(c, ada_w, ada_b.reshape(nl, 1, d6))


def _mm_kernel(a_ref, b_ref, o_ref, *, sigmoid):
    acc = _dot(a_ref[...], b_ref[...])
    if sigmoid:
        acc = jax.nn.sigmoid(acc)
    o_ref[...] = acc.astype(o_ref.dtype)


def matmul(a, b, *, tm, tn, name, out_dtype=BF16, sigmoid=False):
    m, k = a.shape
    n = b.shape[1]
    tm, tn = _tile(m, tm), _tile(n, tn)
    return pl.pallas_call(
        functools.partial(_mm_kernel, sigmoid=sigmoid), name=name,
        grid=(m // tm, n // tn),
        in_specs=[pl.BlockSpec((tm, k), lambda i, j: (i, 0)),
                  pl.BlockSpec((k, tn), lambda i, j: (0, j))],
        out_specs=pl.BlockSpec((tm, tn), lambda i, j: (i, j)),
        out_shape=jax.ShapeDtypeStruct((m, n), out_dtype),
        compiler_params=_params(("parallel", "parallel")),
    )(a, b)


def ssm_matrices(a_re, a_im, log_dt, b_re, b_im, c_re, c_im, d_skip, t_chunk):
    g, n = a_re.shape
    c = b_re.shape[-1]
    dt = jnp.exp(log_dt)[:, None]
    abar_re = jnp.exp(dt * a_re) * jnp.cos(dt * a_im)
    abar_im = jnp.exp(dt * a_re) * jnp.sin(dt * a_im)
    den = jnp.square(a_re) + jnp.square(a_im)
    num_re = abar_re - 1.0
    f_re = (num_re * a_re + abar_im * a_im) / den
    f_im = (abar_im * a_re - num_re * a_im) / den
    bbar_re = f_re[..., None] * b_re - f_im[..., None] * b_im
    bbar_im = f_re[..., None] * b_im + f_im[..., None] * b_re
    k = jnp.arange(t_chunk + 1, dtype=a_re.dtype)[:, None, None]
    pw_mag = jnp.exp(k * (dt * a_re))
    pw_re = pw_mag * jnp.cos(k * (dt * a_im))
    pw_im = pw_mag * jnp.sin(k * (dt * a_im))
    e_re = pw_re[..., None] * bbar_re - pw_im[..., None] * bbar_im
    e_im = pw_re[..., None] * bbar_im + pw_im[..., None] * bbar_re
    kern = (jnp.einsum('gcn,kgnd->gkcd', c_re, e_re[:t_chunk], precision=HIGHEST)
            - jnp.einsum('gcn,kgnd->gkcd', c_im, e_im[:t_chunk], precision=HIGHEST))
    kern = kern.at[:, 0].add(d_skip[:, :, None] * jnp.eye(c, dtype=kern.dtype))
    k_rev = kern[:, ::-1].transpose(0, 2, 1, 3).reshape(g, c, t_chunk * c)
    p_re = e_re[:t_chunk][::-1].transpose(1, 2, 0, 3).reshape(g, n, t_chunk * c)
    p_im = e_im[:t_chunk][::-1].transpose(1, 2, 0, 3).reshape(g, n, t_chunk * c)
    p_t = jnp.concatenate([p_re, p_im], axis=1)
    q_re = (c_re[None] * pw_re[1:, :, None, :] - c_im[None] * pw_im[1:, :, None, :])
    q_im = -(c_re[None] * pw_im[1:, :, None, :] + c_im[None] * pw_re[1:, :, None, :])
    q_t = jnp.concatenate([q_re, q_im], axis=-1).transpose(1, 0, 2, 3).reshape(g, t_chunk * c, 2 * n)
    lam_re, lam_im = pw_re[t_chunk], pw_im[t_chunk]
    lam = jnp.stack([jnp.concatenate([lam_re, lam_re], axis=-1),
                     jnp.concatenate([-lam_im, lam_im], axis=-1)], axis=1)
    return k_rev, p_t, q_t, lam


def _ssm_kernel(x_ref, krev_ref, pt_ref, qt_ref, lam_ref, o_ref, mt_ref, s_ref, ss_ref, h_ref, *, n_steps, bsz):
    c, tc = krev_ref.shape
    k_rev = krev_ref[...]
    base = jnp.concatenate([k_rev, jnp.zeros_like(k_rev)], axis=1)
    for t in range(tc // c):
        shift = tc - (t + 1) * c
        mt_ref[t * c:(t + 1) * c, :] = base[:, shift:shift + tc].astype(BF16)
    xt = x_ref[...].reshape(tc, x_ref.shape[-1])
    s = _dot(pt_ref[...].astype(F32), xt.astype(F32)).T
    half = s.shape[1] // 2
    s_ref[...] = s
    ss_ref[...] = pltpu.roll(s, half, 1)
    a = lam_ref[0:1, :]
    bm = lam_ref[1:2, :]

    def step(j, carry):
        new = []
        for b in range(bsz):
            h, hs = carry[b]
            row = pl.ds(b * n_steps + j, 1)
            h_ref[row, :] = h
            new.append((a * h + bm * hs + s_ref[row, :], a * hs - bm * h + ss_ref[row, :]))
        return tuple(new)

    zero = jnp.zeros((1, 2 * half), F32)
    lax.fori_loop(0, n_steps, step, tuple((zero, zero) for _ in range(bsz)))
    y = _dot(mt_ref[...], xt)
    y = y + _dot(qt_ref[...], h_ref[...].T.astype(BF16))
    o_ref[...] = y.astype(o_ref.dtype).reshape(o_ref.shape)


def ssm_scan(u, mats, bsz, seq, t_chunk):
    k_rev, p_t, q_t, lam = mats
    g, c, tc = k_rev.shape
    n2 = lam.shape[-1]
    n_steps = seq // t_chunk
    chunks = n_steps * bsz
    xt = u.reshape(chunks, t_chunk * g * c).T.reshape(t_chunk, g, c, chunks)
    xspec = pl.BlockSpec((t_chunk, None, c, chunks), lambda i: (0, i, 0, 0))
    wspec = lambda shape: pl.BlockSpec((None,) + shape, lambda i: (i, 0, 0))
    yt = pl.pallas_call(
        functools.partial(_ssm_kernel, n_steps=n_steps, bsz=bsz), name="ssm_scan",
        grid=(g,),
        in_specs=[xspec, wspec((c, tc)), wspec((n2, tc)), wspec((tc, n2)), wspec((2, n2))],
        out_specs=xspec,
        out_shape=jax.ShapeDtypeStruct((t_chunk, g, c, chunks), BF16),
        scratch_shapes=[pltpu.VMEM((tc, tc), BF16)] + [pltpu.VMEM((chunks, n2), F32) for _ in range(3)],
        compiler_params=_params(("parallel",)),
    )(xt, k_rev, p_t.astype(BF16), q_t.astype(BF16), lam)
    return yt.reshape(t_chunk * g * c, chunks).T.reshape(bsz * seq, g * c)


def _layer_norm(z):
    mu = jnp.mean(z, axis=-1, keepdims=True)
    zc = z - mu
    var = jnp.mean(zc * zc, axis=-1, keepdims=True)
    return zc * lax.rsqrt(var + LN_EPS)


def _mixer_kernel(ya_ref, u_ref, v_ref, b_ref, c_ref, h_ref, ch_ref, hh_ref, g0_ref, g1_ref, g2_ref,
                  gluw_ref, glub_ref, lng_ref, lnb_ref, sgw_ref, sgb_ref, cw_ref, wa_ref, wb_ref, wc_ref,
                  o_ref, z_ref, vs_ref, zb_ref, *, tm, seq, chunk, heads):
    z = jax.nn.gelu(ya_ref[...].astype(F32), approximate=True)
    za = z * jax.nn.sigmoid(_dot(z.astype(BF16), gluw_ref[...]) + glub_ref[...])
    merged = g0_ref[...].astype(F32) * _dot(za.astype(BF16), wa_ref[...])

    d_sg = v_ref.shape[-1]
    head_dim = d_sg // heads
    v = (_layer_norm(v_ref[...].astype(F32)) * lng_ref[...] + lnb_ref[...]).astype(BF16)
    col_head = lax.broadcasted_iota(jnp.int32, (chunk, d_sg), 1) // head_dim
    for ci in range(tm // chunk):
        vc = v[ci * chunk:(ci + 1) * chunk, :]
        for hd in range(heads):
            vs_ref[hd * chunk:(hd + 1) * chunk, :] = jnp.where(col_head == hd, vc, jnp.zeros_like(vc))
        s = _dot(sgw_ref[...], vs_ref[...]) + sgb_ref[...]
        zb_ref[ci * chunk:(ci + 1) * chunk, :] = (u_ref[ci * chunk:(ci + 1) * chunk, :].astype(F32) * s).astype(BF16)
    merged = merged + g1_ref[...].astype(F32) * _dot(zb_ref[...], wb_ref[...])

    first = (pl.program_id(0) * tm) % seq == 0
    halo = ch_ref[...].astype(F32) * hh_ref[...].astype(F32)
    z_ref[0:8, :] = jnp.where(first, jnp.zeros_like(halo), halo)
    z_ref[8:, :] = c_ref[...].astype(F32) * h_ref[...].astype(F32)
    conv = (cw_ref[0:1, :] * z_ref[pl.ds(6, tm), :] + cw_ref[1:2, :] * z_ref[pl.ds(7, tm), :]
            + cw_ref[2:3, :] * z_ref[pl.ds(8, tm), :])
    zc = b_ref[...].astype(F32) * conv
    merged = merged + g2_ref[...].astype(F32) * _dot(zc.astype(BF16), wc_ref[...])
    o_ref[...] = merged.astype(o_ref.dtype)


def mixer_tail(ya, p5, gates, glu_w, glu_b, ln_g, ln_b, sg_w, sg_b, conv_w, wa, wb, wc, *, seq, tm):
    m, d_ssm = ya.shape
    d_sg = p5.shape[1] // 5
    d_model = wa.shape[1]
    heads, chunk, _ = sg_w.shape
    tm = _tile(m, tm)
    assert tm % chunk == 0 and seq % tm == 0 and conv_w.shape[0] == 3
    w_low = jnp.where(jnp.tril(jnp.ones((chunk, chunk), dtype=bool))[None], sg_w, 0.0)
    sgw_cat = w_low.transpose(1, 0, 2).reshape(chunk, heads * chunk).astype(BF16)
    sgb_full = jnp.repeat(sg_b.T, d_sg // heads, axis=1)
    seg = lambda k: pl.BlockSpec((tm, d_sg), lambda i: (i, k))
    halo = lambda k: pl.BlockSpec((8, d_sg), lambda i: (jnp.maximum(i * (tm // 8) - 1, 0), k))
    gate = lambda k: pl.BlockSpec((tm, d_model), lambda i: (i, k))
    full = lambda a: pl.BlockSpec(a.shape, lambda i: (0,) * a.ndim)
    consts = [glu_w.astype(BF16), glu_b.reshape(1, -1), ln_g.reshape(1, -1), ln_b.reshape(1, -1),
              sgw_cat, sgb_full, conv_w.reshape(3, -1), wa.astype(BF16), wb.astype(BF16), wc.astype(BF16)]
    return pl.pallas_call(
        functools.partial(_mixer_kernel, tm=tm, seq=seq, chunk=chunk, heads=heads), name="mixer_tail",
        grid=(m // tm,),
        in_specs=[pl.BlockSpec((tm, d_ssm), lambda i: (i, 0)),
                  seg(0), seg(1), seg(2), seg(3), seg(4), halo(3), halo(4),
                  gate(0), gate(1), gate(2)] + [full(a) for a in consts],
        out_specs=pl.BlockSpec((tm, d_model), lambda i: (i, 0)),
        out_shape=jax.ShapeDtypeStruct((m, d_model), BF16),
        scratch_shapes=[pltpu.VMEM((tm + 8, d_sg), F32),
                        pltpu.VMEM((heads * chunk, d_sg), BF16),
                        pltpu.VMEM((tm, d_sg), BF16)],
        compiler_params=_params(("parallel",)),
    )(ya, p5, p5, p5, p5, p5, p5, p5, gates, gates, gates, *consts)


def _post_norm(x, y, sh, sc, g, alpha):
    z = alpha * x + (1.0 + g) * y
    return _layer_norm(z) * (1.0 + sc) + sh


def _mod_specs(tm, seq, d):
    spec = pl.BlockSpec((None, 1, d), lambda i, *_: ((i * tm) // seq, 0, 0))
    return [spec, spec, spec]


def _route_top2(x, rwt_ref, rb_ref, e_ref, w_ref):
    logits = lax.dot_general(rwt_ref[...], x, (((1,), (1,)), ((), ())),
                             preferred_element_type=F32, precision=HIGHEST) + rb_ref[...]
    n_exp = logits.shape[0]
    eid = lax.broadcasted_iota(jnp.int32, logits.shape, 0)
    m1 = jnp.max(logits, axis=0, keepdims=True)
    e1 = jnp.min(jnp.where(logits == m1, eid, n_exp), axis=0, keepdims=True)
    rest = jnp.where(eid == e1, -jnp.inf, logits)
    m2 = jnp.max(rest, axis=0, keepdims=True)
    e2 = jnp.min(jnp.where(rest == m2, eid, n_exp), axis=0, keepdims=True)
    ex = jnp.exp(m2 - m1)
    e_ref[0:1, :] = e1
    e_ref[1:2, :] = e2
    w_ref[0:1, :] = 1.0 / (1.0 + ex)
    w_ref[1:2, :] = ex / (1.0 + ex)


def _outproj_kernel(a_ref, w_ref, x_ref, sh_ref, sc_ref, g_ref, o32_ref, o16_ref, *, alpha):
    y = _dot(a_ref[...], w_ref[...])
    out = _post_norm(x_ref[...], y, sh_ref[...], sc_ref[...], g_ref[...], alpha)
    o32_ref[...] = out
    o16_ref[...] = out.astype(BF16)


def _outproj_route_kernel(a_ref, w_ref, x_ref, sh_ref, sc_ref, g_ref, rwt_ref, rb_ref, o32_ref, e_ref, tw_ref, *, alpha):
    y = _dot(a_ref[...], w_ref[...])
    out = _post_norm(x_ref[...], y, sh_ref[...], sc_ref[...], g_ref[...], alpha)
    o32_ref[...] = out
    _route_top2(out, rwt_ref, rb_ref, e_ref, tw_ref)


def outproj_postnorm(a, w_o, x, sh, sc, g, *, seq, alpha, tm, router=None):
    m, d = x.shape
    tm = _tile(m, tm)
    assert seq % tm == 0
    row = pl.BlockSpec((tm, d), lambda i: (i, 0))
    in_specs = [pl.BlockSpec((tm, a.shape[1]), lambda i: (i, 0)),
                pl.BlockSpec(w_o.shape, lambda i: (0, 0)), row] + _mod_specs(tm, seq, d)
    if router is None:
        return pl.pallas_call(
            functools.partial(_outproj_kernel, alpha=alpha), name="outproj_postnorm",
            grid=(m // tm,),
            in_specs=in_specs,
            out_specs=[row, row],
            out_shape=[jax.ShapeDtypeStruct((m, d), F32), jax.ShapeDtypeStruct((m, d), BF16)],
            compiler_params=_params(("parallel",)),
        )(a, w_o, x, sh, sc, g)
    router_w, router_b = router
    n_exp = router_w.shape[1]
    top = pl.BlockSpec((TOP_K, tm), lambda i: (0, i))
    return pl.pallas_call(
        functools.partial(_outproj_route_kernel, alpha=alpha), name="outproj_postnorm_route",
        grid=(m // tm,),
        in_specs=in_specs + [pl.BlockSpec((n_exp, d), lambda i: (0, 0)), pl.BlockSpec((n_exp, 1), lambda i: (0, 0))],
        out_specs=[row, top, top],
        out_shape=[jax.ShapeDtypeStruct((m, d), F32), jax.ShapeDtypeStruct((TOP_K, m), jnp.int32),
                   jax.ShapeDtypeStruct((TOP_K, m), F32)],
        compiler_params=_params(("parallel",)),
    )(a, w_o, x, sh, sc, g, router_w.T, router_b.reshape(n_exp, 1))


def _swiglu_step(xb, w1_ref, w3_ref, w2_ref, acc_ref):
    gate = _dot(xb, w1_ref[...])
    up = _dot(xb, w3_ref[...])
    act = (gate * jax.nn.sigmoid(gate) * up).astype(BF16)
    acc_ref[...] += _dot(act, w2_ref[...])


def _cast_split(shape, n_blocks):
    e, r, c = shape
    if n_blocks % e:
        return None
    per = n_blocks // e
    for cb in (8, 4, 2, 1):
        if per % cb == 0 and c % (cb * 128) == 0 and r % ((per // cb) * 16) == 0:
            return per // cb, cb
    return None


def _ffn_kernel(xb_ref, w1_ref, w3_ref, w2_ref, x_ref, sh_ref, sc_ref, g_ref, *rest, alpha, n_cast, cast_steps):
    cast_in, (o32_ref, o16_ref), cast_out = rest[:n_cast], rest[n_cast:n_cast + 2], rest[n_cast + 2:]
    f = pl.program_id(1)

    @pl.when(f == 0)
    def _():
        o32_ref[...] = jnp.zeros_like(o32_ref)

    @pl.when(f < cast_steps)
    def _():
        for src, dst in zip(cast_in, cast_out):
            dst[...] = src[...].astype(BF16)

    _swiglu_step(xb_ref[...], w1_ref, w3_ref, w2_ref, o32_ref)

    @pl.when(f == pl.num_programs(1) - 1)
    def _():
        out = _post_norm(x_ref[...], o32_ref[...], sh_ref[...], sc_ref[...], g_ref[...], alpha)
        o32_ref[...] = out
        o16_ref[...] = out.astype(BF16)


def ffn_postnorm(xb, x, w13, w2, sh, sc, g, *, seq, alpha, tm, tf, cast=()):
    m, d = x.shape
    ffn = w2.shape[0]
    tm, tf = _tile(m, tm), _tile(ffn, tf)
    nf = ffn // tf
    assert seq % tm == 0
    cast_steps = 1 << (nf.bit_length() - 1)
    splits = [_cast_split(a.shape, (m // tm) * cast_steps) for a in cast]
    in_kernel = [a for a, sp in zip(cast, splits) if sp is not None]
    cast_specs = []
    for a, (rb, cb) in [(a, sp) for a, sp in zip(cast, splits) if sp is not None]:
        def index(i, f, rb=rb, cb=cb):
            t = i * cast_steps + jnp.minimum(f, cast_steps - 1)
            return t // (rb * cb), (t // cb) % rb, t % cb
        cast_specs.append(pl.BlockSpec((None, a.shape[1] // rb, a.shape[2] // cb), index))
    row = pl.BlockSpec((tm, d), lambda i, f: (i, 0))
    outs = pl.pallas_call(
        functools.partial(_ffn_kernel, alpha=alpha, n_cast=len(in_kernel), cast_steps=cast_steps), name="ffn_postnorm",
        grid=(m // tm, nf),
        in_specs=[row,
                  pl.BlockSpec((d, tf), lambda i, f: (0, f)),
                  pl.BlockSpec((d, tf), lambda i, f: (0, nf + f)),
                  pl.BlockSpec((tf, d), lambda i, f: (f, 0)),
                  row] + _mod_specs(tm, seq, d) + cast_specs,
        out_specs=[row, row] + cast_specs,
        out_shape=[jax.ShapeDtypeStruct((m, d), F32), jax.ShapeDtypeStruct((m, d), BF16)]
                  + [jax.ShapeDtypeStruct(a.shape, BF16) for a in in_kernel],
        compiler_params=_params(("parallel", "arbitrary")),
    )(xb, w13, w13, w2, x, sh, sc, g, *in_kernel)
    done = iter(outs[2:])
    return outs[0], outs[1], [next(done) if sp is not None else a.astype(BF16) for a, sp in zip(cast, splits)]


def _rank_kernel(e_ref, rank_ref, count_ref, carry_ref, *, n_exp):
    @pl.when(pl.program_id(0) == 0)
    def _():
        carry_ref[...] = jnp.zeros_like(carry_ref)

    e = e_ref[...]
    tb = e.shape[1]
    eid = lax.broadcasted_iota(jnp.int32, (n_exp, tb), 0)
    hit = (eid == e[0:1, :]) | (eid == e[1:2, :])
    onehot = hit.astype(F32)
    upper = (lax.broadcasted_iota(jnp.int32, (tb, tb), 0) <= lax.broadcasted_iota(jnp.int32, (tb, tb), 1))
    incl = _dot(hit.astype(BF16), upper.astype(BF16))
    excl = incl - onehot + carry_ref[...]
    r0 = jnp.sum(jnp.where(eid == e[0:1, :], excl, 0.0), axis=0, keepdims=True)
    r1 = jnp.sum(jnp.where(eid == e[1:2, :], excl, 0.0), axis=0, keepdims=True)
    rank_ref[0:1, :] = r0.astype(jnp.int32)
    rank_ref[1:2, :] = r1.astype(jnp.int32)
    carry_ref[...] += jnp.sum(onehot, axis=1, keepdims=True)
    count_ref[...] = carry_ref[...].astype(jnp.int32)


def expert_ranks(e_idx, n_exp, *, tb):
    m = e_idx.shape[1]
    tb = _tile(m, tb)
    blk = pl.BlockSpec((TOP_K, tb), lambda i: (0, i))
    return pl.pallas_call(
        functools.partial(_rank_kernel, n_exp=n_exp), name="expert_ranks",
        grid=(m // tb,),
        in_specs=[blk],
        out_specs=[blk, pl.BlockSpec((n_exp, 1), lambda i: (0, 0))],
        out_shape=[jax.ShapeDtypeStruct((TOP_K, m), jnp.int32), jax.ShapeDtypeStruct((n_exp, 1), jnp.int32)],
        scratch_shapes=[pltpu.VMEM((n_exp, 1), F32)],
        compiler_params=_params(("arbitrary",)),
    )(e_idx)


def _dispatch_kernel(pad_ref, dest_ref, x_ref, xs_ref, zero_ref, sem, zsem):
    tm, d = x_ref.shape
    n_pad = pad_ref.shape[1]

    @pl.when(pl.program_id(0) == 0)
    def _():
        zero_ref[...] = jnp.zeros_like(zero_ref)

        def zero_copy(r):
            return pltpu.make_async_copy(zero_ref, xs_ref.at[pl.ds(r, 1), :], zsem)

        for e in range(n_pad):
            lax.fori_loop(pad_ref[0, e], pad_ref[1, e], lambda r, c: (zero_copy(r).start(), c)[1], 0)
        for e in range(n_pad):
            lax.fori_loop(pad_ref[0, e], pad_ref[1, e], lambda r, c: (zero_copy(r).wait(), c)[1], 0)

    def issue(r, carry):
        for k in range(TOP_K):
            pltpu.make_async_copy(x_ref.at[pl.ds(r, 1), :], xs_ref.at[pl.ds(dest_ref[0, k, r], 1), :], sem).start()
        return carry

    lax.fori_loop(0, tm, issue, 0, unroll=8)
    for k in range(TOP_K):
        pltpu.make_async_copy(x_ref, xs_ref.at[pl.ds(0, tm), :], sem).wait()


def dispatch_rows(x, dest, pad_bounds, n_rows, *, tm):
    m, d = x.shape
    tm = _tile(m, tm)
    dest_t = dest.reshape(TOP_K, m // tm, tm).transpose(1, 0, 2)
    return pl.pallas_call(
        _dispatch_kernel, name="dispatch_rows",
        grid=(m // tm,),
        in_specs=[pl.BlockSpec(memory_space=pltpu.SMEM),
                  pl.BlockSpec((1, TOP_K, tm), lambda i: (i, 0, 0), memory_space=pltpu.SMEM),
                  pl.BlockSpec((tm, d), lambda i: (i, 0))],
        out_specs=pl.BlockSpec(memory_space=pl.ANY),
        out_shape=jax.ShapeDtypeStruct((n_rows, d), x.dtype),
        scratch_shapes=[pltpu.VMEM((1, d), x.dtype), pltpu.SemaphoreType.DMA(()), pltpu.SemaphoreType.DMA(())],
        compiler_params=_params(("arbitrary",)),
    )(pad_bounds, dest_t, x)


def _expert_kernel(be_ref, nv_ref, rb_ref, xs_ref, w1_ref, w3_ref, w2_ref, ys_ref, xb_ref):
    del be_ref, rb_ref
    b, f = pl.program_id(0), pl.program_id(1)
    used = nv_ref[b] > 0

    @pl.when(f == 0)
    def _():
        ys_ref[...] = jnp.zeros_like(ys_ref)

    @pl.when(used & (f == 0))
    def _():
        xb_ref[...] = xs_ref[...].astype(BF16)

    @pl.when(used)
    def _():
        _swiglu_step(xb_ref[...], w1_ref, w3_ref, w2_ref, ys_ref)


def expert_blocks(xs, w13, w2, block_e, block_nv, block_row, *, rows, tf):
    n_rows, d = xs.shape
    n_exp, ffn, _ = w2.shape
    tf = _tile(ffn, tf)
    nf = ffn // tf
    nb = n_rows // rows
    fi = lambda b, f, nv: jnp.where(nv[b] > 0, f, nf - 1)
    grid_spec = pltpu.PrefetchScalarGridSpec(
        num_scalar_prefetch=3,
        grid=(nb, nf),
        in_specs=[pl.BlockSpec((rows, d), lambda b, f, be, nv, rb: (rb[b], 0)),
                  pl.BlockSpec((None, d, tf), lambda b, f, be, nv, rb: (be[b], 0, fi(b, f, nv))),
                  pl.BlockSpec((None, d, tf), lambda b, f, be, nv, rb: (be[b], 0, nf + fi(b, f, nv))),
                  pl.BlockSpec((None, tf, d), lambda b, f, be, nv, rb: (be[b], fi(b, f, nv), 0))],
        out_specs=pl.BlockSpec((rows, d), lambda b, f, be, nv, rb: (b, 0)),
        scratch_shapes=[pltpu.VMEM((rows, d), BF16)])
    return pl.pallas_call(
        _expert_kernel, name="expert_blocks",
        grid_spec=grid_spec,
        out_shape=jax.ShapeDtypeStruct((n_rows, d), F32),
        compiler_params=_params(("arbitrary", "arbitrary")),
    )(block_e, block_nv, block_row, xs, w13, w13, w2)


def _combine_kernel(dest_ref, dnext_ref, ys_ref, w0_ref, w1_ref, x_ref, sh_ref, sc_ref, g_ref, o_ref, buf_ref, sems, *, alpha):
    i, n = pl.program_id(0), pl.num_programs(0)
    tm = x_ref.shape[0]
    slot = i % 2

    def gather(d_ref, s):
        def issue(r, carry):
            for k in range(TOP_K):
                pltpu.make_async_copy(ys_ref.at[pl.ds(d_ref[0, k, r], 1), :],
                                      buf_ref.at[s, k, pl.ds(r, 1), :], sems.at[s]).start()
            return carry
        lax.fori_loop(0, tm, issue, 0, unroll=8)

    @pl.when(i == 0)
    def _():
        gather(dest_ref, 0)

    @pl.when(i + 1 < n)
    def _():
        gather(dnext_ref, 1 - slot)

    for k in range(TOP_K):
        pltpu.make_async_copy(ys_ref.at[pl.ds(0, tm), :], buf_ref.at[slot, k], sems.at[slot]).wait()
    y = w0_ref[...] * buf_ref[slot, 0] + w1_ref[...] * buf_ref[slot, 1]
    o_ref[...] = _post_norm(x_ref[...], y, sh_ref[...], sc_ref[...], g_ref[...], alpha)


def combine_postnorm(ys, dest, top_w, x, sh, sc, g, *, seq, alpha, tm):
    m, d = x.shape
    tm = _tile(m, tm)
    assert seq % tm == 0
    n = m // tm
    dest_t = dest.reshape(TOP_K, n, tm).transpose(1, 0, 2)
    row = pl.BlockSpec((tm, d), lambda i: (i, 0))
    col = pl.BlockSpec((tm, 1), lambda i: (i, 0))
    return pl.pallas_call(
        functools.partial(_combine_kernel, alpha=alpha), name="combine_postnorm",
        grid=(n,),
        in_specs=[pl.BlockSpec((1, TOP_K, tm), lambda i: (i, 0, 0), memory_space=pltpu.SMEM),
                  pl.BlockSpec((1, TOP_K, tm), lambda i: (jnp.minimum(i + 1, n - 1), 0, 0), memory_space=pltpu.SMEM),
                  pl.BlockSpec(memory_space=pl.ANY), col, col, row] + _mod_specs(tm, seq, d),
        out_specs=row,
        out_shape=jax.ShapeDtypeStruct((m, d), F32),
        scratch_shapes=[pltpu.VMEM((2, TOP_K, tm, d), F32), pltpu.SemaphoreType.DMA((2,))],
        compiler_params=_params(("arbitrary",)),
    )(dest_t, dest_t, ys, top_w[0].reshape(m, 1), top_w[1].reshape(m, 1), x, sh, sc, g)


def moe_postnorm(x, e_idx, top_w, w13, w2, sh, sc, g, *, seq, alpha):
    m, d = x.shape
    n_exp = w2.shape[0]
    rank, counts = expert_ranks(e_idx, n_exp, tb=512)
    counts = counts[:, 0]
    padded = ((counts + MOE_ROWS - 1) // MOE_ROWS) * MOE_ROWS
    ends = jnp.cumsum(padded)
    starts = ends - padded
    expert_id = jnp.arange(n_exp, dtype=jnp.int32)[:, None, None]
    dest = rank + jnp.sum(jnp.where(e_idx[None] == expert_id, starts[:, None, None], 0), axis=0)
    dest = dest.astype(jnp.int32)
    n_rows = ((TOP_K * m + MOE_ROWS - 1) // MOE_ROWS) * MOE_ROWS + n_exp * MOE_ROWS
    pad_bounds = jnp.stack([jnp.append(starts + counts, ends[-1]), jnp.append(ends, n_rows)]).astype(jnp.int32)
    nb = n_rows // MOE_ROWS
    block_start = jnp.arange(nb, dtype=jnp.int32) * MOE_ROWS
    last_used = jnp.maximum(ends[-1] // MOE_ROWS - 1, 0)
    block_e = jnp.minimum(jnp.sum(block_start[:, None] >= ends[None, :], axis=1), n_exp - 1).astype(jnp.int32)
    block_e = jnp.where(block_start < ends[-1], block_e, block_e[last_used])
    block_nv = jnp.clip(counts[block_e] - (block_start - starts[block_e]), 0, MOE_ROWS)
    block_nv = jnp.where(block_start < ends[-1], block_nv, 0).astype(jnp.int32)
    block_row = jnp.minimum(jnp.arange(nb, dtype=jnp.int32), last_used).astype(jnp.int32)
    xs = dispatch_rows(x, dest, pad_bounds, n_rows, tm=512)
    ys = expert_blocks(xs, w13, w2, block_e, block_nv, block_row, rows=MOE_ROWS, tf=1024)
    return combine_postnorm(ys, dest, top_w, x, sh, sc, g, seq=seq, alpha=alpha, tm=512)


def kernel(x, c, w_in, ssm_a_re, ssm_a_im, ssm_log_dt, ssm_b_re, ssm_b_im, ssm_c_re, ssm_c_im, ssm_d, glu_w, glu_b, sg_ln_g, sg_ln_b, sg_w, sg_b, conv_w, w_branch_a, w_branch_b, w_branch_c, w_o, ada_w, ada_b, ffn_w13, ffn_w2, moe_router_w, moe_router_b, moe_w13, moe_w2):
    bsz, seq, d = x.shape
    depth = w_in.shape[0]
    m = bsz * seq
    d_ssm = glu_w.shape[1]
    d_sg = sg_ln_g.shape[1]
    alpha = (2.0 * depth) ** 0.25
    assert conv_w.shape[-1] == d_sg and w_in.shape[2] == d_ssm + 5 * d_sg + 3 * d

    mod = ada_modulation(c, ada_w, ada_b)
    ssm_mats = jax.vmap(functools.partial(ssm_matrices, t_chunk=SSM_CHUNK))(
        ssm_a_re, ssm_a_im, ssm_log_dt, ssm_b_re, ssm_b_im, ssm_c_re, ssm_c_im, ssm_d)
    moe_bf16 = None
    x32 = x.reshape(m, d)
    xb = x32.astype(BF16)
    for l in range(depth):
        sh_m, sc_m, g_m, sh_f, sc_f, g_f = [mod[l, :, k * d:(k + 1) * d].reshape(bsz, 1, d) for k in range(6)]
        w_ssm, w_p5, w_gates = [w_in[l, :, a:b].astype(BF16) for a, b in
                                ((0, d_ssm), (d_ssm, d_ssm + 5 * d_sg), (d_ssm + 5 * d_sg, w_in.shape[2]))]
        u_ssm = matmul(xb, w_ssm, tm=1024, tn=512, name="inproj_ssm")
        p5 = matmul(xb, w_p5, tm=1024, tn=1280, name="inproj_gmlp_conv")
        gates = matmul(xb, w_gates, tm=1024, tn=2048, sigmoid=True, name="inproj_gates")
        y_ssm = ssm_scan(u_ssm, [a[l] for a in ssm_mats], bsz, seq, SSM_CHUNK)
        merged = mixer_tail(y_ssm, p5, gates, glu_w[l], glu_b[l], sg_ln_g[l], sg_ln_b[l], sg_w[l], sg_b[l],
                            conv_w[l], w_branch_a[l], w_branch_b[l], w_branch_c[l], seq=seq, tm=256)
        if l % 2 == 0:
            x32, xb = outproj_postnorm(merged, w_o[l].astype(BF16), x32, sh_m, sc_m, g_m, seq=seq, alpha=alpha, tm=512)
            nxt = [moe_w13[(l + 1) // 2], moe_w2[(l + 1) // 2]] if l + 1 < depth else []
            x32, xb, moe_bf16 = ffn_postnorm(xb, x32, ffn_w13[l // 2].astype(BF16), ffn_w2[l // 2].astype(BF16),
                                             sh_f, sc_f, g_f, seq=seq, alpha=alpha, tm=512, tf=512, cast=nxt)
        else:
            x32, e_idx, top_w = outproj_postnorm(merged, w_o[l].astype(BF16), x32, sh_m, sc_m, g_m, seq=seq,
                                                 alpha=alpha, tm=512, router=(moe_router_w[l // 2], moe_router_b[l // 2]))
            w13_b, w2_b = moe_bf16 or (moe_w13[l // 2].astype(BF16), moe_w2[l // 2].astype(BF16))
            x32 = moe_postnorm(x32, e_idx, top_w, w13_b, w2_b, sh_f, sc_f, g_f, seq=seq, alpha=alpha)
            xb = x32.astype(BF16)
    return x32.reshape(bsz, seq, d)
```

```python
import functools

import jax
import jax.numpy as jnp
from jax import lax
from jax.experimental import pallas as pl
from jax.experimental.pallas import tpu as pltpu

BF16 = jnp.bfloat16
F32 = jnp.float32

LN_EPS = 1e-5
TOP_K = 2
SSM_CHUNK = 32
MOE_ROWS = 512
V7X_VMEM_LIMIT = 56 * 1024 * 1024
HIGHEST = lax.Precision.HIGHEST


def _tile(dim, pref):
    if dim <= pref:
        return dim
    for t in range(pref - pref % 128, 0, -128):
        if dim % t == 0:
            return t
    raise ValueError((dim, pref))


def _params(sem):
    return pltpu.CompilerParams(dimension_semantics=sem, vmem_limit_bytes=V7X_VMEM_LIMIT)


def _dot(a, b):
    return jnp.dot(a, b, preferred_element_type=F32)


def _ada_kernel(c_ref, w_ref, b_ref, o_ref):
    o_ref[...] = jnp.dot(c_ref[...], w_ref[...], preferred_element_type=F32, precision=HIGHEST) + b_ref[...]


def ada_modulation(c, ada_w, ada_b):
    nl, d, d6 = ada_w.shape
    bsz = c.shape[0]
    tn = _tile(d6, 1024)
    return pl.pallas_call(
        _ada_kernel, name="ada_modulation",
        grid=(nl, d6 // tn),
        in_specs=[pl.BlockSpec((bsz, d), lambda l, j: (0, 0)),
                  pl.BlockSpec((None, d, tn), lambda l, j: (l, 0, j)),
                  pl.BlockSpec((None, 1, tn), lambda l, j: (l, 0, j))],
        out_specs=pl.BlockSpec((None, bsz, tn), lambda l, j: (l, 0, j)),
        out_shape=jax.ShapeDtypeStruct((nl, bsz, d6), F32),
        compiler_params=_params(("parallel", "parallel")),
    )(c, ada_w, ada_b.reshape(nl, 1, d6))


def _mm_kernel(a_ref, b_ref, o_ref, *, sigmoid):
    acc = _dot(a_ref[...], b_ref[...])
    if sigmoid:
        acc = jax.nn.sigmoid(acc)
    o_ref[...] = acc.astype(o_ref.dtype)


def matmul(a, b, *, tm, tn, name, out_dtype=BF16, sigmoid=False):
    m, k = a.shape
    n = b.shape[1]
    tm, tn = _tile(m, tm), _tile(n, tn)
    return pl.pallas_call(
        functools.partial(_mm_kernel, sigmoid=sigmoid), name=name,
        grid=(m // tm, n // tn),
        in_specs=[pl.BlockSpec((tm, k), lambda i, j: (i, 0)),
                  pl.BlockSpec((k, tn), lambda i, j: (0, j))],
        out_specs=pl.BlockSpec((tm, tn), lambda i, j: (i, j)),
        out_shape=jax.ShapeDtypeStruct((m, n), out_dtype),
        compiler_params=_params(("parallel", "parallel")),
    )(a, b)


def _mm_cast_kernel(a_ref, b_ref, o_ref, a16_ref):
    a16 = a_ref[...].astype(BF16)
    a16_ref[...] = a16
    o_ref[...] = _dot(a16, b_ref[...]).astype(o_ref.dtype)


def matmul_cast(a, b, *, tm, name):
    m, k = a.shape
    n = b.shape[1]
    tm = _tile(m, tm)
    return pl.pallas_call(
        _mm_cast_kernel, name=name,
        grid=(m // tm,),
        in_specs=[pl.BlockSpec((tm, k), lambda i: (i, 0)),
                  pl.BlockSpec((k, n), lambda i: (0, 0))],
        out_specs=[pl.BlockSpec((tm, n), lambda i: (i, 0)), pl.BlockSpec((tm, k), lambda i: (i, 0))],
        out_shape=[jax.ShapeDtypeStruct((m, n), BF16), jax.ShapeDtypeStruct((m, k), BF16)],
        compiler_params=_params(("parallel",)),
    )(a, b)


def _transpose_kernel(x_ref, o_ref):
    o_ref[...] = x_ref[...].astype(F32).T.astype(o_ref.dtype)


def transpose2d(x, *, tile, name):
    r, c = x.shape
    if c >= r:
        tc = _tile(c, tile)
        grid, in_spec, out_spec = (c // tc,), pl.BlockSpec((r, tc), lambda j: (0, j)), pl.BlockSpec((tc, r), lambda j: (j, 0))
    else:
        tr = _tile(r, tile)
        grid, in_spec, out_spec = (r // tr,), pl.BlockSpec((tr, c), lambda i: (i, 0)), pl.BlockSpec((c, tr), lambda i: (0, i))
    return pl.pallas_call(
        _transpose_kernel, name=name,
        grid=grid,
        in_specs=[in_spec],
        out_specs=out_spec,
        out_shape=jax.ShapeDtypeStruct((c, r), x.dtype),
        compiler_params=_params(("parallel",)),
    )(x)


def ssm_matrices(a_re, a_im, log_dt, b_re, b_im, c_re, c_im, d_skip, t_chunk):
    g, n = a_re.shape
    c = b_re.shape[-1]
    dt = jnp.exp(log_dt)[:, None]
    abar_re = jnp.exp(dt * a_re) * jnp.cos(dt * a_im)
    abar_im = jnp.exp(dt * a_re) * jnp.sin(dt * a_im)
    den = jnp.square(a_re) + jnp.square(a_im)
    num_re = abar_re - 1.0
    f_re = (num_re * a_re + abar_im * a_im) / den
    f_im = (abar_im * a_re - num_re * a_im) / den
    bbar_re = f_re[..., None] * b_re - f_im[..., None] * b_im
    bbar_im = f_re[..., None] * b_im + f_im[..., None] * b_re
    k = jnp.arange(t_chunk + 1, dtype=a_re.dtype)[:, None, None]
    pw_mag = jnp.exp(k * (dt * a_re))
    pw_re = pw_mag * jnp.cos(k * (dt * a_im))
    pw_im = pw_mag * jnp.sin(k * (dt * a_im))
    e_re = pw_re[..., None] * bbar_re - pw_im[..., None] * bbar_im
    e_im = pw_re[..., None] * bbar_im + pw_im[..., None] * bbar_re
    kern = (jnp.einsum('gcn,kgnd->gkcd', c_re, e_re[:t_chunk], precision=HIGHEST)
            - jnp.einsum('gcn,kgnd->gkcd', c_im, e_im[:t_chunk], precision=HIGHEST))
    kern = kern.at[:, 0].add(d_skip[:, :, None] * jnp.eye(c, dtype=kern.dtype))
    k_rev = kern[:, ::-1].transpose(0, 2, 1, 3).reshape(g, c, t_chunk * c)
    p_re = e_re[:t_chunk][::-1].transpose(1, 2, 0, 3).reshape(g, n, t_chunk * c)
    p_im = e_im[:t_chunk][::-1].transpose(1, 2, 0, 3).reshape(g, n, t_chunk * c)
    p_t = jnp.concatenate([p_re, p_im], axis=1)
    q_re = (c_re[None] * pw_re[1:, :, None, :] - c_im[None] * pw_im[1:, :, None, :])
    q_im = -(c_re[None] * pw_im[1:, :, None, :] + c_im[None] * pw_re[1:, :, None, :])
    q_t = jnp.concatenate([q_re, q_im], axis=-1).transpose(1, 0, 2, 3).reshape(g, t_chunk * c, 2 * n)
    lam_re, lam_im = pw_re[t_chunk], pw_im[t_chunk]
    lam = jnp.stack([jnp.concatenate([lam_re, lam_re], axis=-1),
                     jnp.concatenate([-lam_im, lam_im], axis=-1)], axis=1)
    return k_rev, p_t, q_t, lam


def _ssm_kernel(x_ref, krev_ref, pt_ref, qt_ref, lam_ref, o_ref, mt_ref, s_ref, ss_ref, h_ref, *, n_steps, bsz):
    c, tc = krev_ref.shape
    k_rev = krev_ref[...]
    base = jnp.concatenate([k_rev, jnp.zeros_like(k_rev)], axis=1)
    for t in range(tc // c):
        shift = tc - (t + 1) * c
        mt_ref[t * c:(t + 1) * c, :] = base[:, shift:shift + tc].astype(BF16)
    xt = x_ref[...].reshape(tc, x_ref.shape[-1])
    s = _dot(pt_ref[...].astype(F32), xt.astype(F32)).T
    half = s.shape[1] // 2
    s_ref[...] = s
    ss_ref[...] = pltpu.roll(s, half, 1)
    a = lam_ref[0:1, :]
    bm = lam_ref[1:2, :]

    def step(j, carry):
        new = []
        for b in range(bsz):
            h, hs = carry[b]
            row = pl.ds(b * n_steps + j, 1)
            h_ref[row, :] = h
            new.append((a * h + bm * hs + s_ref[row, :], a * hs - bm * h + ss_ref[row, :]))
        return tuple(new)

    zero = jnp.zeros((1, 2 * half), F32)
    lax.fori_loop(0, n_steps, step, tuple((zero, zero) for _ in range(bsz)))
    y = _dot(mt_ref[...], xt)
    y = y + _dot(qt_ref[...], h_ref[...].T.astype(BF16))
    o_ref[...] = y.astype(o_ref.dtype).reshape(o_ref.shape)


def ssm_scan(u, mats, bsz, seq, t_chunk):
    k_rev, p_t, q_t, lam = mats
    g, c, tc = k_rev.shape
    n2 = lam.shape[-1]
    n_steps = seq // t_chunk
    chunks = n_steps * bsz
    xt = transpose2d(u.reshape(chunks, t_chunk * g * c), tile=2048, name="ssm_to_lanes").reshape(t_chunk, g, c, chunks)
    xspec = pl.BlockSpec((t_chunk, None, c, chunks), lambda i: (0, i, 0, 0))
    wspec = lambda shape: pl.BlockSpec((None,) + shape, lambda i: (i, 0, 0))
    yt = pl.pallas_call(
        functools.partial(_ssm_kernel, n_steps=n_steps, bsz=bsz), name="ssm_scan",
        grid=(g,),
        in_specs=[xspec, wspec((c, tc)), wspec((n2, tc)), wspec((tc, n2)), wspec((2, n2))],
        out_specs=xspec,
        out_shape=jax.ShapeDtypeStruct((t_chunk, g, c, chunks), BF16),
        scratch_shapes=[pltpu.VMEM((tc, tc), BF16)] + [pltpu.VMEM((chunks, n2), F32) for _ in range(3)],
        compiler_params=_params(("parallel",)),
    )(xt, k_rev, p_t.astype(BF16), q_t.astype(BF16), lam)
    return transpose2d(yt.reshape(t_chunk * g * c, chunks), tile=2048, name="ssm_from_lanes").reshape(bsz * seq, g * c)


def _layer_norm(z):
    mu = jnp.mean(z, axis=-1, keepdims=True)
    zc = z - mu
    var = jnp.mean(zc * zc, axis=-1, keepdims=True)
    return zc * lax.rsqrt(var + LN_EPS)


def _mixer_kernel(ya_ref, u_ref, v_ref, b_ref, c_ref, h_ref, ch_ref, hh_ref, g0_ref, g1_ref, g2_ref,
                  gluw_ref, glub_ref, lng_ref, lnb_ref, sgw_ref, sgb_ref, cw_ref, wa_ref, wb_ref, wc_ref,
                  o_ref, z_ref, vs_ref, zb_ref, *, tm, seq, chunk, heads):
    z = jax.nn.gelu(ya_ref[...].astype(F32), approximate=True)
    za = z * jax.nn.sigmoid(_dot(z.astype(BF16), gluw_ref[...]) + glub_ref[...])
    merged = g0_ref[...].astype(F32) * _dot(za.astype(BF16), wa_ref[...])

    d_sg = v_ref.shape[-1]
    head_dim = d_sg // heads
    v = (_layer_norm(v_ref[...].astype(F32)) * lng_ref[...] + lnb_ref[...]).astype(BF16)
    col_head = lax.broadcasted_iota(jnp.int32, (chunk, d_sg), 1) // head_dim
    for ci in range(tm // chunk):
        vc = v[ci * chunk:(ci + 1) * chunk, :]
        for hd in range(heads):
            vs_ref[hd * chunk:(hd + 1) * chunk, :] = jnp.where(col_head == hd, vc, jnp.zeros_like(vc))
        s = _dot(sgw_ref[...], vs_ref[...]) + sgb_ref[...]
        zb_ref[ci * chunk:(ci + 1) * chunk, :] = (u_ref[ci * chunk:(ci + 1) * chunk, :].astype(F32) * s).astype(BF16)
    merged = merged + g1_ref[...].astype(F32) * _dot(zb_ref[...], wb_ref[...])

    first = (pl.program_id(0) * tm) % seq == 0
    halo = ch_ref[...].astype(F32) * hh_ref[...].astype(F32)
    z_ref[0:8, :] = jnp.where(first, jnp.zeros_like(halo), halo)
    z_ref[8:, :] = c_ref[...].astype(F32) * h_ref[...].astype(F32)
    conv = (cw_ref[0:1, :] * z_ref[pl.ds(6, tm), :] + cw_ref[1:2, :] * z_ref[pl.ds(7, tm), :]
            + cw_ref[2:3, :] * z_ref[pl.ds(8, tm), :])
    zc = b_ref[...].astype(F32) * conv
    merged = merged + g2_ref[...].astype(F32) * _dot(zc.astype(BF16), wc_ref[...])
    o_ref[...] = merged.astype(o_ref.dtype)


def mixer_tail(ya, p5, gates, glu_w, glu_b, ln_g, ln_b, sg_w, sg_b, conv_w, wa, wb, wc, *, seq, tm):
    m, d_ssm = ya.shape
    d_sg = p5.shape[1] // 5
    d_model = wa.shape[1]
    heads, chunk, _ = sg_w.shape
    tm = _tile(m, tm)
    assert tm % chunk == 0 and seq % tm == 0 and conv_w.shape[0] == 3
    w_low = jnp.where(jnp.tril(jnp.ones((chunk, chunk), dtype=bool))[None], sg_w, 0.0)
    sgw_cat = w_low.transpose(1, 0, 2).reshape(chunk, heads * chunk).astype(BF16)
    sgb_full = jnp.repeat(sg_b.T, d_sg // heads, axis=1)
    seg = lambda k: pl.BlockSpec((tm, d_sg), lambda i: (i, k))
    halo = lambda k: pl.BlockSpec((8, d_sg), lambda i: (jnp.maximum(i * (tm // 8) - 1, 0), k))
    gate = lambda k: pl.BlockSpec((tm, d_model), lambda i: (i, k))
    full = lambda a: pl.BlockSpec(a.shape, lambda i: (0,) * a.ndim)
    consts = [glu_w.astype(BF16), glu_b.reshape(1, -1), ln_g.reshape(1, -1), ln_b.reshape(1, -1),
              sgw_cat, sgb_full, conv_w.reshape(3, -1), wa.astype(BF16), wb.astype(BF16), wc.astype(BF16)]
    return pl.pallas_call(
        functools.partial(_mixer_kernel, tm=tm, seq=seq, chunk=chunk, heads=heads), name="mixer_tail",
        grid=(m // tm,),
        in_specs=[pl.BlockSpec((tm, d_ssm), lambda i: (i, 0)),
                  seg(0), seg(1), seg(2), seg(3), seg(4), halo(3), halo(4),
                  gate(0), gate(1), gate(2)] + [full(a) for a in consts],
        out_specs=pl.BlockSpec((tm, d_model), lambda i: (i, 0)),
        out_shape=jax.ShapeDtypeStruct((m, d_model), BF16),
        scratch_shapes=[pltpu.VMEM((tm + 8, d_sg), F32),
                        pltpu.VMEM((heads * chunk, d_sg), BF16),
                        pltpu.VMEM((tm, d_sg), BF16)],
        compiler_params=_params(("parallel",)),
    )(ya, p5, p5, p5, p5, p5, p5, p5, gates, gates, gates, *consts)


def _post_norm(x, y, sh, sc, g, alpha):
    z = alpha * x + (1.0 + g) * y
    return _layer_norm(z) * (1.0 + sc) + sh


def _mod_specs(tm, seq, d):
    spec = pl.BlockSpec((None, 1, d), lambda i, *_: ((i * tm) // seq, 0, 0))
    return [spec, spec, spec]


def _route_top2(x, rwt_ref, rb_ref, e_ref, w_ref):
    logits = lax.dot_general(rwt_ref[...], x, (((1,), (1,)), ((), ())),
                             preferred_element_type=F32, precision=HIGHEST) + rb_ref[...]
    n_exp = logits.shape[0]
    eid = lax.broadcasted_iota(jnp.int32, logits.shape, 0)
    m1 = jnp.max(logits, axis=0, keepdims=True)
    e1 = jnp.min(jnp.where(logits == m1, eid, n_exp), axis=0, keepdims=True)
    rest = jnp.where(eid == e1, -jnp.inf, logits)
    m2 = jnp.max(rest, axis=0, keepdims=True)
    e2 = jnp.min(jnp.where(rest == m2, eid, n_exp), axis=0, keepdims=True)
    ex = jnp.exp(m2 - m1)
    e_ref[0:1, :] = e1
    e_ref[1:2, :] = e2
    w_ref[0:1, :] = 1.0 / (1.0 + ex)
    w_ref[1:2, :] = ex / (1.0 + ex)


def _outproj_kernel(a_ref, w_ref, x_ref, sh_ref, sc_ref, g_ref, o32_ref, o16_ref, *, alpha):
    y = _dot(a_ref[...], w_ref[...])
    out = _post_norm(x_ref[...], y, sh_ref[...], sc_ref[...], g_ref[...], alpha)
    o32_ref[...] = out
    o16_ref[...] = out.astype(BF16)


def _outproj_route_kernel(a_ref, w_ref, x_ref, sh_ref, sc_ref, g_ref, rwt_ref, rb_ref, o32_ref, e_ref, tw_ref, *, alpha):
    y = _dot(a_ref[...], w_ref[...])
    out = _post_norm(x_ref[...], y, sh_ref[...], sc_ref[...], g_ref[...], alpha)
    o32_ref[...] = out
    _route_top2(out, rwt_ref, rb_ref, e_ref, tw_ref)


def outproj_postnorm(a, w_o, x, sh, sc, g, *, seq, alpha, tm, router=None):
    m, d = x.shape
    tm = _tile(m, tm)
    assert seq % tm == 0
    row = pl.BlockSpec((tm, d), lambda i: (i, 0))
    in_specs = [pl.BlockSpec((tm, a.shape[1]), lambda i: (i, 0)),
                pl.BlockSpec(w_o.shape, lambda i: (0, 0)), row] + _mod_specs(tm, seq, d)
    if router is None:
        return pl.pallas_call(
            functools.partial(_outproj_kernel, alpha=alpha), name="outproj_postnorm",
            grid=(m // tm,),
            in_specs=in_specs,
            out_specs=[row, row],
            out_shape=[jax.ShapeDtypeStruct((m, d), F32), jax.ShapeDtypeStruct((m, d), BF16)],
            compiler_params=_params(("parallel",)),
        )(a, w_o, x, sh, sc, g)
    router_w, router_b = router
    n_exp = router_w.shape[1]
    top = pl.BlockSpec((TOP_K, tm), lambda i: (0, i))
    return pl.pallas_call(
        functools.partial(_outproj_route_kernel, alpha=alpha), name="outproj_postnorm_route",
        grid=(m // tm,),
        in_specs=in_specs + [pl.BlockSpec((n_exp, d), lambda i: (0, 0)), pl.BlockSpec((n_exp, 1), lambda i: (0, 0))],
        out_specs=[row, top, top],
        out_shape=[jax.ShapeDtypeStruct((m, d), F32), jax.ShapeDtypeStruct((TOP_K, m), jnp.int32),
                   jax.ShapeDtypeStruct((TOP_K, m), F32)],
        compiler_params=_params(("parallel",)),
    )(a, w_o, x, sh, sc, g, router_w.T, router_b.reshape(n_exp, 1))


def _swiglu_step(xb, w1_ref, w3_ref, w2_ref, acc_ref):
    gate = _dot(xb, w1_ref[...])
    up = _dot(xb, w3_ref[...])
    act = (gate * jax.nn.sigmoid(gate) * up).astype(BF16)
    acc_ref[...] += _dot(act, w2_ref[...])


def _cast_split(shape, n_blocks):
    e, r, c = shape
    if n_blocks % e:
        return None
    per = n_blocks // e
    for cb in (8, 4, 2, 1):
        if per % cb == 0 and c % (cb * 128) == 0 and r % ((per // cb) * 16) == 0:
            return per // cb, cb
    return None


def _ffn_kernel(xb_ref, w1_ref, w3_ref, w2_ref, x_ref, sh_ref, sc_ref, g_ref, *rest, alpha, n_cast, cast_steps):
    cast_in, (o32_ref, o16_ref), cast_out = rest[:n_cast], rest[n_cast:n_cast + 2], rest[n_cast + 2:]
    f = pl.program_id(1)

    @pl.when(f == 0)
    def _():
        o32_ref[...] = jnp.zeros_like(o32_ref)

    @pl.when(f < cast_steps)
    def _():
        for src, dst in zip(cast_in, cast_out):
            dst[...] = src[...].astype(BF16)

    _swiglu_step(xb_ref[...], w1_ref, w3_ref, w2_ref, o32_ref)

    @pl.when(f == pl.num_programs(1) - 1)
    def _():
        out = _post_norm(x_ref[...], o32_ref[...], sh_ref[...], sc_ref[...], g_ref[...], alpha)
        o32_ref[...] = out
        o16_ref[...] = out.astype(BF16)


def ffn_postnorm(xb, x, w13, w2, sh, sc, g, *, seq, alpha, tm, tf, cast=()):
    m, d = x.shape
    ffn = w2.shape[0]
    tm, tf = _tile(m, tm), _tile(ffn, tf)
    nf = ffn // tf
    assert seq % tm == 0
    cast_steps = 1 << (nf.bit_length() - 1)
    splits = [_cast_split(a.shape, (m // tm) * cast_steps) for a in cast]
    in_kernel = [a for a, sp in zip(cast, splits) if sp is not None]
    cast_specs = []
    for a, (rb, cb) in [(a, sp) for a, sp in zip(cast, splits) if sp is not None]:
        def index(i, f, rb=rb, cb=cb):
            t = i * cast_steps + jnp.minimum(f, cast_steps - 1)
            return t // (rb * cb), (t // cb) % rb, t % cb
        cast_specs.append(pl.BlockSpec((None, a.shape[1] // rb, a.shape[2] // cb), index))
    row = pl.BlockSpec((tm, d), lambda i, f: (i, 0))
    outs = pl.pallas_call(
        functools.partial(_ffn_kernel, alpha=alpha, n_cast=len(in_kernel), cast_steps=cast_steps), name="ffn_postnorm",
        grid=(m // tm, nf),
        in_specs=[row,
                  pl.BlockSpec((d, tf), lambda i, f: (0, f)),
                  pl.BlockSpec((d, tf), lambda i, f: (0, nf + f)),
                  pl.BlockSpec((tf, d), lambda i, f: (f, 0)),
                  row] + _mod_specs(tm, seq, d) + cast_specs,
        out_specs=[row, row] + cast_specs,
        out_shape=[jax.ShapeDtypeStruct((m, d), F32), jax.ShapeDtypeStruct((m, d), BF16)]
                  + [jax.ShapeDtypeStruct(a.shape, BF16) for a in in_kernel],
        compiler_params=_params(("parallel", "arbitrary")),
    )(xb, w13, w13, w2, x, sh, sc, g, *in_kernel)
    done = iter(outs[2:])
    return outs[0], outs[1], [next(done) if sp is not None else a.astype(BF16) for a, sp in zip(cast, splits)]


def _rank_kernel(e_ref, rank_ref, count_ref, carry_ref, *, n_exp):
    @pl.when(pl.program_id(0) == 0)
    def _():
        carry_ref[...] = jnp.zeros_like(carry_ref)

    e = e_ref[...]
    tb = e.shape[1]
    eid = lax.broadcasted_iota(jnp.int32, (n_exp, tb), 0)
    hit = (eid == e[0:1, :]) | (eid == e[1:2, :])
    onehot = hit.astype(F32)
    upper = (lax.broadcasted_iota(jnp.int32, (tb, tb), 0) <= lax.broadcasted_iota(jnp.int32, (tb, tb), 1))
    incl = _dot(hit.astype(BF16), upper.astype(BF16))
    excl = incl - onehot + carry_ref[...]
    r0 = jnp.sum(jnp.where(eid == e[0:1, :], excl, 0.0), axis=0, keepdims=True)
    r1 = jnp.sum(jnp.where(eid == e[1:2, :], excl, 0.0), axis=0, keepdims=True)
    rank_ref[0:1, :] = r0.astype(jnp.int32)
    rank_ref[1:2, :] = r1.astype(jnp.int32)
    carry_ref[...] += jnp.sum(onehot, axis=1, keepdims=True)
    count_ref[...] = carry_ref[...].astype(jnp.int32)


def expert_ranks(e_idx, n_exp, *, tb):
    m = e_idx.shape[1]
    tb = _tile(m, tb)
    blk = pl.BlockSpec((TOP_K, tb), lambda i: (0, i))
    return pl.pallas_call(
        functools.partial(_rank_kernel, n_exp=n_exp), name="expert_ranks",
        grid=(m // tb,),
        in_specs=[blk],
        out_specs=[blk, pl.BlockSpec((n_exp, 1), lambda i: (0, 0))],
        out_shape=[jax.ShapeDtypeStruct((TOP_K, m), jnp.int32), jax.ShapeDtypeStruct((n_exp, 1), jnp.int32)],
        scratch_shapes=[pltpu.VMEM((n_exp, 1), F32)],
        compiler_params=_params(("arbitrary",)),
    )(e_idx)


def _dispatch_kernel(pad_ref, dest_ref, x_ref, xs_ref, zero_ref, sem, zsem):
    tm, d = x_ref.shape
    n_pad = pad_ref.shape[1]

    @pl.when(pl.program_id(0) == 0)
    def _():
        zero_ref[...] = jnp.zeros_like(zero_ref)

        def zero_copy(r):
            return pltpu.make_async_copy(zero_ref, xs_ref.at[pl.ds(r, 1), :], zsem)

        for e in range(n_pad):
            lax.fori_loop(pad_ref[0, e], pad_ref[1, e], lambda r, c: (zero_copy(r).start(), c)[1], 0)
        for e in range(n_pad):
            lax.fori_loop(pad_ref[0, e], pad_ref[1, e], lambda r, c: (zero_copy(r).wait(), c)[1], 0)

    def issue(r, carry):
        for k in range(TOP_K):
            pltpu.make_async_copy(x_ref.at[pl.ds(r, 1), :], xs_ref.at[pl.ds(dest_ref[0, k, r], 1), :], sem).start()
        return carry

    lax.fori_loop(0, tm, issue, 0, unroll=8)
    for k in range(TOP_K):
        pltpu.make_async_copy(x_ref, xs_ref.at[pl.ds(0, tm), :], sem).wait()


def dispatch_rows(x, dest, pad_bounds, n_rows, *, tm):
    m, d = x.shape
    tm = _tile(m, tm)
    dest_t = dest.reshape(TOP_K, m // tm, tm).transpose(1, 0, 2)
    return pl.pallas_call(
        _dispatch_kernel, name="dispatch_rows",
        grid=(m // tm,),
        in_specs=[pl.BlockSpec(memory_space=pltpu.SMEM),
                  pl.BlockSpec((1, TOP_K, tm), lambda i: (i, 0, 0), memory_space=pltpu.SMEM),
                  pl.BlockSpec((tm, d), lambda i: (i, 0))],
        out_specs=pl.BlockSpec(memory_space=pl.ANY),
        out_shape=jax.ShapeDtypeStruct((n_rows, d), x.dtype),
        scratch_shapes=[pltpu.VMEM((1, d), x.dtype), pltpu.SemaphoreType.DMA(()), pltpu.SemaphoreType.DMA(())],
        compiler_params=_params(("arbitrary",)),
    )(pad_bounds, dest_t, x)


def _expert_kernel(be_ref, nv_ref, rb_ref, xs_ref, w1_ref, w3_ref, w2_ref, ys_ref, xb_ref):
    del be_ref, rb_ref
    b, f = pl.program_id(0), pl.program_id(1)
    used = nv_ref[b] > 0

    @pl.when(f == 0)
    def _():
        ys_ref[...] = jnp.zeros_like(ys_ref)

    @pl.when(used & (f == 0))
    def _():
        xb_ref[...] = xs_ref[...].astype(BF16)

    @pl.when(used)
    def _():
        _swiglu_step(xb_ref[...], w1_ref, w3_ref, w2_ref, ys_ref)


def expert_blocks(xs, w13, w2, block_e, block_nv, block_row, *, rows, tf):
    n_rows, d = xs.shape
    n_exp, ffn, _ = w2.shape
    tf = _tile(ffn, tf)
    nf = ffn // tf
    nb = n_rows // rows
    fi = lambda b, f, nv: jnp.where(nv[b] > 0, f, nf - 1)
    grid_spec = pltpu.PrefetchScalarGridSpec(
        num_scalar_prefetch=3,
        grid=(nb, nf),
        in_specs=[pl.BlockSpec((rows, d), lambda b, f, be, nv, rb: (rb[b], 0)),
                  pl.BlockSpec((None, d, tf), lambda b, f, be, nv, rb: (be[b], 0, fi(b, f, nv))),
                  pl.BlockSpec((None, d, tf), lambda b, f, be, nv, rb: (be[b], 0, nf + fi(b, f, nv))),
                  pl.BlockSpec((None, tf, d), lambda b, f, be, nv, rb: (be[b], fi(b, f, nv), 0))],
        out_specs=pl.BlockSpec((rows, d), lambda b, f, be, nv, rb: (b, 0)),
        scratch_shapes=[pltpu.VMEM((rows, d), BF16)])
    return pl.pallas_call(
        _expert_kernel, name="expert_blocks",
        grid_spec=grid_spec,
        out_shape=jax.ShapeDtypeStruct((n_rows, d), F32),
        compiler_params=_params(("arbitrary", "arbitrary")),
    )(block_e, block_nv, block_row, xs, w13, w13, w2)


def _combine_kernel(dest_ref, dnext_ref, ys_ref, w0_ref, w1_ref, x_ref, sh_ref, sc_ref, g_ref, o_ref, buf_ref, sems, *, alpha):
    i, n = pl.program_id(0), pl.num_programs(0)
    tm = x_ref.shape[0]
    slot = i % 2

    def gather(d_ref, s):
        def issue(r, carry):
            for k in range(TOP_K):
                pltpu.make_async_copy(ys_ref.at[pl.ds(d_ref[0, k, r], 1), :],
                                      buf_ref.at[s, k, pl.ds(r, 1), :], sems.at[s]).start()
            return carry
        lax.fori_loop(0, tm, issue, 0, unroll=8)

    @pl.when(i == 0)
    def _():
        gather(dest_ref, 0)

    @pl.when(i + 1 < n)
    def _():
        gather(dnext_ref, 1 - slot)

    for k in range(TOP_K):
        pltpu.make_async_copy(ys_ref.at[pl.ds(0, tm), :], buf_ref.at[slot, k], sems.at[slot]).wait()
    y = w0_ref[...] * buf_ref[slot, 0] + w1_ref[...] * buf_ref[slot, 1]
    o_ref[...] = _post_norm(x_ref[...], y, sh_ref[...], sc_ref[...], g_ref[...], alpha)


def combine_postnorm(ys, dest, top_w, x, sh, sc, g, *, seq, alpha, tm):
    m, d = x.shape
    tm = _tile(m, tm)
    assert seq % tm == 0
    n = m // tm
    dest_t = dest.reshape(TOP_K, n, tm).transpose(1, 0, 2)
    row = pl.BlockSpec((tm, d), lambda i: (i, 0))
    col = pl.BlockSpec((tm, 1), lambda i: (i, 0))
    return pl.pallas_call(
        functools.partial(_combine_kernel, alpha=alpha), name="combine_postnorm",
        grid=(n,),
        in_specs=[pl.BlockSpec((1, TOP_K, tm), lambda i: (i, 0, 0), memory_space=pltpu.SMEM),
                  pl.BlockSpec((1, TOP_K, tm), lambda i: (jnp.minimum(i + 1, n - 1), 0, 0), memory_space=pltpu.SMEM),
                  pl.BlockSpec(memory_space=pl.ANY), col, col, row] + _mod_specs(tm, seq, d),
        out_specs=row,
        out_shape=jax.ShapeDtypeStruct((m, d), F32),
        scratch_shapes=[pltpu.VMEM((2, TOP_K, tm, d), F32), pltpu.SemaphoreType.DMA((2,))],
        compiler_params=_params(("arbitrary",)),
    )(dest_t, dest_t, ys, top_w[0].reshape(m, 1), top_w[1].reshape(m, 1), x, sh, sc, g)


def moe_postnorm(x, e_idx, top_w, w13, w2, sh, sc, g, *, seq, alpha):
    m, d = x.shape
    n_exp = w2.shape[0]
    rank, counts = expert_ranks(e_idx, n_exp, tb=512)
    counts = counts[:, 0]
    padded = ((counts + MOE_ROWS - 1) // MOE_ROWS) * MOE_ROWS
    ends = jnp.cumsum(padded)
    starts = ends - padded
    expert_id = jnp.arange(n_exp, dtype=jnp.int32)[:, None, None]
    dest = rank + jnp.sum(jnp.where(e_idx[None] == expert_id, starts[:, None, None], 0), axis=0)
    dest = dest.astype(jnp.int32)
    n_rows = ((TOP_K * m + MOE_ROWS - 1) // MOE_ROWS) * MOE_ROWS + n_exp * MOE_ROWS
    pad_bounds = jnp.stack([jnp.append(starts + counts, ends[-1]), jnp.append(ends, n_rows)]).astype(jnp.int32)
    nb = n_rows // MOE_ROWS
    block_start = jnp.arange(nb, dtype=jnp.int32) * MOE_ROWS
    last_used = jnp.maximum(ends[-1] // MOE_ROWS - 1, 0)
    block_e = jnp.minimum(jnp.sum(block_start[:, None] >= ends[None, :], axis=1), n_exp - 1).astype(jnp.int32)
    block_e = jnp.where(block_start < ends[-1], block_e, block_e[last_used])
    block_nv = jnp.clip(counts[block_e] - (block_start - starts[block_e]), 0, MOE_ROWS)
    block_nv = jnp.where(block_start < ends[-1], block_nv, 0).astype(jnp.int32)
    block_row = jnp.minimum(jnp.arange(nb, dtype=jnp.int32), last_used).astype(jnp.int32)
    xs = dispatch_rows(x, dest, pad_bounds, n_rows, tm=512)
    ys = expert_blocks(xs, w13, w2, block_e, block_nv, block_row, rows=MOE_ROWS, tf=1024)
    return combine_postnorm(ys, dest, top_w, x, sh, sc, g, seq=seq, alpha=alpha, tm=512)


def kernel(x, c, w_in, ssm_a_re, ssm_a_im, ssm_log_dt, ssm_b_re, ssm_b_im, ssm_c_re, ssm_c_im, ssm_d, glu_w, glu_b, sg_ln_g, sg_ln_b, sg_w, sg_b, conv_w, w_branch_a, w_branch_b, w_branch_c, w_o, ada_w, ada_b, ffn_w13, ffn_w2, moe_router_w, moe_router_b, moe_w13, moe_w2):
    bsz, seq, d = x.shape
    depth = w_in.shape[0]
    m = bsz * seq
    d_ssm = glu_w.shape[1]
    d_sg = sg_ln_g.shape[1]
    alpha = (2.0 * depth) ** 0.25
    assert conv_w.shape[-1] == d_sg and w_in.shape[2] == d_ssm + 5 * d_sg + 3 * d

    mod = ada_modulation(c, ada_w, ada_b)
    ssm_mats = jax.vmap(functools.partial(ssm_matrices, t_chunk=SSM_CHUNK))(
        ssm_a_re, ssm_a_im, ssm_log_dt, ssm_b_re, ssm_b_im, ssm_c_re, ssm_c_im, ssm_d)
    moe_bf16 = None
    x32 = x.reshape(m, d)
    xb = None
    for l in range(depth):
        sh_m, sc_m, g_m, sh_f, sc_f, g_f = [mod[l, :, k * d:(k + 1) * d].reshape(bsz, 1, d) for k in range(6)]
        w_ssm, w_p5, w_gates = [w_in[l, :, a:b].astype(BF16) for a, b in
                                ((0, d_ssm), (d_ssm, d_ssm + 5 * d_sg), (d_ssm + 5 * d_sg, w_in.shape[2]))]
        if xb is None:
            u_ssm, xb = matmul_cast(x32, w_ssm, tm=1024, name="inproj_ssm_cast")
        else:
            u_ssm = matmul(xb, w_ssm, tm=1024, tn=512, name="inproj_ssm")
        p5 = matmul(xb, w_p5, tm=1024, tn=1280, name="inproj_gmlp_conv")
        gates = matmul(xb, w_gates, tm=1024, tn=2048, sigmoid=True, name="inproj_gates")
        y_ssm = ssm_scan(u_ssm, [a[l] for a in ssm_mats], bsz, seq, SSM_CHUNK)
        merged = mixer_tail(y_ssm, p5, gates, glu_w[l], glu_b[l], sg_ln_g[l], sg_ln_b[l], sg_w[l], sg_b[l],
                            conv_w[l], w_branch_a[l], w_branch_b[l], w_branch_c[l], seq=seq, tm=256)
        if l % 2 == 0:
            x32, xb = outproj_postnorm(merged, w_o[l].astype(BF16), x32, sh_m, sc_m, g_m, seq=seq, alpha=alpha, tm=512)
            nxt = [moe_w13[(l + 1) // 2], moe_w2[(l + 1) // 2]] if l + 1 < depth else []
            x32, xb, moe_bf16 = ffn_postnorm(xb, x32, ffn_w13[l // 2].astype(BF16), ffn_w2[l // 2].astype(BF16),
                                             sh_f, sc_f, g_f, seq=seq, alpha=alpha, tm=512, tf=512, cast=nxt)
        else:
            x32, e_idx, top_w = outproj_postnorm(merged, w_o[l].astype(BF16), x32, sh_m, sc_m, g_m, seq=seq,
                                                 alpha=alpha, tm=512, router=(moe_router_w[l // 2], moe_router_b[l // 2]))
            w13_b, w2_b = moe_bf16 or (moe_w13[l // 2].astype(BF16), moe_w2[l // 2].astype(BF16))
            x32 = moe_postnorm(x32, e_idx, top_w, w13_b, w2_b, sh_f, sc_f, g_f, seq=seq, alpha=alpha)
            xb = None
    return x32.reshape(bsz, seq, d)
```

```python
import functools

import jax
import jax.numpy as jnp
from jax import lax
from jax.experimental import pallas as pl
from jax.experimental.pallas import tpu as pltpu

BF16 = jnp.bfloat16
F32 = jnp.float32

LN_EPS = 1e-5
TOP_K = 2
SSM_CHUNK = 32
MOE_ROWS = 512
V7X_VMEM_LIMIT = 56 * 1024 * 1024
HIGHEST = lax.Precision.HIGHEST


def _tile(dim, pref):
    if dim <= pref:
        return dim
    for t in range(pref - pref % 128, 0, -128):
        if dim % t == 0:
            return t
    raise ValueError((dim, pref))


def _params(sem):
    return pltpu.CompilerParams(dimension_semantics=sem, vmem_limit_bytes=V7X_VMEM_LIMIT)


def _dot(a, b):
    return jnp.dot(a, b, preferred_element_type=F32)


def _ada_kernel(c_ref, w_ref, b_ref, o_ref):
    o_ref[...] = jnp.dot(c_ref[...], w_ref[...], preferred_element_type=F32, precision=HIGHEST) + b_ref[...]


def ada_modulation(c, ada_w, ada_b):
    nl, d, d6 = ada_w.shape
    bsz = c.shape[0]
    tn = _tile(d6, 1024)
    return pl.pallas_call(
        _ada_kernel, name="ada_modulation",
        grid=(nl, d6 // tn),
        in_specs=[pl.BlockSpec((bsz, d), lambda l, j: (0, 0)),
                  pl.BlockSpec((None, d, tn), lambda l, j: (l, 0, j)),
                  pl.BlockSpec((None, 1, tn), lambda l, j: (l, 0, j))],
        out_specs=pl.BlockSpec((None, bsz, tn), lambda l, j: (l, 0, j)),
        out_shape=jax.ShapeDtypeStruct((nl, bsz, d6), F32),
        compiler_params=_params(("parallel", "parallel")),
    )(c, ada_w, ada_b.reshape(nl, 1, d6))


def _mm_kernel(a_ref, b_ref, o_ref, *, sigmoid):
    acc = _dot(a_ref[...], b_ref[...])
    if sigmoid:
        acc = jax.nn.sigmoid(acc)
    o_ref[...] = acc.astype(o_ref.dtype)


def matmul(a, b, *, tm, tn, name, out_dtype=BF16, sigmoid=False):
    m, k = a.shape
    n = b.shape[1]
    tm, tn = _tile(m, tm), _tile(n, tn)
    return pl.pallas_call(
        functools.partial(_mm_kernel, sigmoid=sigmoid), name=name,
        grid=(m // tm, n // tn),
        in_specs=[pl.BlockSpec((tm, k), lambda i, j: (i, 0)),
                  pl.BlockSpec((k, tn), lambda i, j: (0, j))],
        out_specs=pl.BlockSpec((tm, tn), lambda i, j: (i, j)),
        out_shape=jax.ShapeDtypeStruct((m, n), out_dtype),
        compiler_params=_params(("parallel", "parallel")),
    )(a, b)


def _mm_cast_kernel(a_ref, b_ref, o_ref, a16_ref):
    a16 = a_ref[...].astype(BF16)
    a16_ref[...] = a16
    o_ref[...] = _dot(a16, b_ref[...]).astype(o_ref.dtype)


def matmul_cast(a, b, *, tm, name):
    m, k = a.shape
    n = b.shape[1]
    tm = _tile(m, tm)
    return pl.pallas_call(
        _mm_cast_kernel, name=name,
        grid=(m // tm,),
        in_specs=[pl.BlockSpec((tm, k), lambda i: (i, 0)),
                  pl.BlockSpec((k, n), lambda i: (0, 0))],
        out_specs=[pl.BlockSpec((tm, n), lambda i: (i, 0)), pl.BlockSpec((tm, k), lambda i: (i, 0))],
        out_shape=[jax.ShapeDtypeStruct((m, n), BF16), jax.ShapeDtypeStruct((m, k), BF16)],
        compiler_params=_params(("parallel",)),
    )(a, b)


def _to_lanes_kernel(u_ref, o_ref, uf_ref, xc_ref, *, t_chunk):
    w, chunks = u_ref.shape[1], o_ref.shape[1]
    for k in range(w // 128):
        uf_ref[k] = u_ref[:, k * 128:(k + 1) * 128].astype(F32)
    for s in range(t_chunk):
        for k in range(w // 128):
            xc_ref[:, s * w + k * 128:s * w + (k + 1) * 128] = uf_ref[k, pl.ds(s, chunks, stride=t_chunk), :]
    o_ref[...] = xc_ref[...].T.astype(o_ref.dtype)


def _from_lanes_kernel(y_ref, o_ref, yc_ref, of_ref, *, t_chunk):
    w, chunks = o_ref.shape[1], y_ref.shape[1]
    yc_ref[...] = y_ref[...].astype(F32).T
    for s in range(t_chunk):
        for k in range(w // 128):
            of_ref[k, pl.ds(s, chunks, stride=t_chunk), :] = yc_ref[:, s * w + k * 128:s * w + (k + 1) * 128]
    for k in range(w // 128):
        o_ref[:, k * 128:(k + 1) * 128] = of_ref[k].astype(o_ref.dtype)


def chunks_to_lanes(u, t_chunk, *, tile):
    m, w = u.shape
    assert w % 128 == 0
    chunks = m // t_chunk
    tc = _tile(chunks, tile)
    return pl.pallas_call(
        functools.partial(_to_lanes_kernel, t_chunk=t_chunk), name="ssm_to_lanes",
        grid=(chunks // tc,),
        in_specs=[pl.BlockSpec((tc * t_chunk, w), lambda i: (i, 0))],
        out_specs=pl.BlockSpec((t_chunk * w, tc), lambda i: (0, i)),
        out_shape=jax.ShapeDtypeStruct((t_chunk * w, chunks), u.dtype),
        scratch_shapes=[pltpu.VMEM((w // 128, tc * t_chunk, 128), F32), pltpu.VMEM((tc, t_chunk * w), F32)],
        compiler_params=_params(("parallel",)),
    )(u)


def lanes_to_chunks(yt, t_chunk, *, tile):
    tw, chunks = yt.shape
    w = tw // t_chunk
    tc = _tile(chunks, tile)
    return pl.pallas_call(
        functools.partial(_from_lanes_kernel, t_chunk=t_chunk), name="ssm_from_lanes",
        grid=(chunks // tc,),
        in_specs=[pl.BlockSpec((tw, tc), lambda i: (0, i))],
        out_specs=pl.BlockSpec((tc * t_chunk, w), lambda i: (i, 0)),
        out_shape=jax.ShapeDtypeStruct((chunks * t_chunk, w), yt.dtype),
        scratch_shapes=[pltpu.VMEM((tc, tw), F32), pltpu.VMEM((w // 128, tc * t_chunk, 128), F32)],
        compiler_params=_params(("parallel",)),
    )(yt)


def ssm_matrices(a_re, a_im, log_dt, b_re, b_im, c_re, c_im, d_skip, t_chunk):
    g, n = a_re.shape
    c = b_re.shape[-1]
    dt = jnp.exp(log_dt)[:, None]
    abar_re = jnp.exp(dt * a_re) * jnp.cos(dt * a_im)
    abar_im = jnp.exp(dt * a_re) * jnp.sin(dt * a_im)
    den = jnp.square(a_re) + jnp.square(a_im)
    num_re = abar_re - 1.0
    f_re = (num_re * a_re + abar_im * a_im) / den
    f_im = (abar_im * a_re - num_re * a_im) / den
    bbar_re = f_re[..., None] * b_re - f_im[..., None] * b_im
    bbar_im = f_re[..., None] * b_im + f_im[..., None] * b_re
    k = jnp.arange(t_chunk + 1, dtype=a_re.dtype)[:, None, None]
    pw_mag = jnp.exp(k * (dt * a_re))
    pw_re = pw_mag * jnp.cos(k * (dt * a_im))
    pw_im = pw_mag * jnp.sin(k * (dt * a_im))
    e_re = pw_re[..., None] * bbar_re - pw_im[..., None] * bbar_im
    e_im = pw_re[..., None] * bbar_im + pw_im[..., None] * bbar_re
    kern = (jnp.einsum('gcn,kgnd->gkcd', c_re, e_re[:t_chunk], precision=HIGHEST)
            - jnp.einsum('gcn,kgnd->gkcd', c_im, e_im[:t_chunk], precision=HIGHEST))
    kern = kern.at[:, 0].add(d_skip[:, :, None] * jnp.eye(c, dtype=kern.dtype))
    k_rev = kern[:, ::-1].transpose(0, 2, 1, 3).reshape(g, c, t_chunk * c)
    p_re = e_re[:t_chunk][::-1].transpose(1, 2, 0, 3).reshape(g, n, t_chunk * c)
    p_im = e_im[:t_chunk][::-1].transpose(1, 2, 0, 3).reshape(g, n, t_chunk * c)
    p_t = jnp.concatenate([p_re, p_im], axis=1)
    q_re = (c_re[None] * pw_re[1:, :, None, :] - c_im[None] * pw_im[1:, :, None, :])
    q_im = -(c_re[None] * pw_im[1:, :, None, :] + c_im[None] * pw_re[1:, :, None, :])
    q_t = jnp.concatenate([q_re, q_im], axis=-1).transpose(1, 0, 2, 3).reshape(g, t_chunk * c, 2 * n)
    lam_re, lam_im = pw_re[t_chunk], pw_im[t_chunk]
    lam = jnp.stack([jnp.concatenate([lam_re, lam_re], axis=-1),
                     jnp.concatenate([-lam_im, lam_im], axis=-1)], axis=1)
    return k_rev, p_t, q_t, lam


def _ssm_kernel(x_ref, krev_ref, pt_ref, qt_ref, lam_ref, o_ref, mt_ref, s_ref, ss_ref, h_ref, *, n_steps, bsz):
    c, tc = krev_ref.shape
    k_rev = krev_ref[...]
    base = jnp.concatenate([k_rev, jnp.zeros_like(k_rev)], axis=1)
    for t in range(tc // c):
        shift = tc - (t + 1) * c
        mt_ref[t * c:(t + 1) * c, :] = base[:, shift:shift + tc].astype(BF16)
    xt = x_ref[...].reshape(tc, x_ref.shape[-1])
    s = _dot(pt_ref[...].astype(F32), xt.astype(F32)).T
    half = s.shape[1] // 2
    s_ref[...] = s
    ss_ref[...] = pltpu.roll(s, half, 1)
    a = lam_ref[0:1, :]
    bm = lam_ref[1:2, :]

    def step(j, carry):
        new = []
        for b in range(bsz):
            h, hs = carry[b]
            row = pl.ds(b * n_steps + j, 1)
            h_ref[row, :] = h
            new.append((a * h + bm * hs + s_ref[row, :], a * hs - bm * h + ss_ref[row, :]))
        return tuple(new)

    zero = jnp.zeros((1, 2 * half), F32)
    lax.fori_loop(0, n_steps, step, tuple((zero, zero) for _ in range(bsz)))
    y = _dot(mt_ref[...], xt)
    y = y + _dot(qt_ref[...], h_ref[...].T.astype(BF16))
    o_ref[...] = y.astype(o_ref.dtype).reshape(o_ref.shape)


def ssm_scan(u, mats, bsz, seq, t_chunk):
    k_rev, p_t, q_t, lam = mats
    g, c, tc = k_rev.shape
    n2 = lam.shape[-1]
    n_steps = seq // t_chunk
    chunks = n_steps * bsz
    xt = chunks_to_lanes(u, t_chunk, tile=128).reshape(t_chunk, g, c, chunks)
    xspec = pl.BlockSpec((t_chunk, None, c, chunks), lambda i: (0, i, 0, 0))
    wspec = lambda shape: pl.BlockSpec((None,) + shape, lambda i: (i, 0, 0))
    yt = pl.pallas_call(
        functools.partial(_ssm_kernel, n_steps=n_steps, bsz=bsz), name="ssm_scan",
        grid=(g,),
        in_specs=[xspec, wspec((c, tc)), wspec((n2, tc)), wspec((tc, n2)), wspec((2, n2))],
        out_specs=xspec,
        out_shape=jax.ShapeDtypeStruct((t_chunk, g, c, chunks), BF16),
        scratch_shapes=[pltpu.VMEM((tc, tc), BF16)] + [pltpu.VMEM((chunks, n2), F32) for _ in range(3)],
        compiler_params=_params(("parallel",)),
    )(xt, k_rev, p_t.astype(BF16), q_t.astype(BF16), lam)
    return lanes_to_chunks(yt.reshape(t_chunk * g * c, chunks), t_chunk, tile=128)


def _layer_norm(z):
    mu = jnp.mean(z, axis=-1, keepdims=True)
    zc = z - mu
    var = jnp.mean(zc * zc, axis=-1, keepdims=True)
    return zc * lax.rsqrt(var + LN_EPS)


def _mixer_kernel(ya_ref, u_ref, v_ref, b_ref, c_ref, h_ref, ch_ref, hh_ref, g0_ref, g1_ref, g2_ref,
                  gluw_ref, glub_ref, lng_ref, lnb_ref, sgw_ref, sgb_ref, cw_ref, wa_ref, wb_ref, wc_ref,
                  o_ref, z_ref, vs_ref, zb_ref, *, tm, seq, chunk, heads):
    z = jax.nn.gelu(ya_ref[...].astype(F32), approximate=True)
    za = z * jax.nn.sigmoid(_dot(z.astype(BF16), gluw_ref[...]) + glub_ref[...])
    merged = g0_ref[...].astype(F32) * _dot(za.astype(BF16), wa_ref[...])

    d_sg = v_ref.shape[-1]
    head_dim = d_sg // heads
    v = (_layer_norm(v_ref[...].astype(F32)) * lng_ref[...] + lnb_ref[...]).astype(BF16)
    col_head = lax.broadcasted_iota(jnp.int32, (chunk, d_sg), 1) // head_dim
    for ci in range(tm // chunk):
        vc = v[ci * chunk:(ci + 1) * chunk, :]
        for hd in range(heads):
            vs_ref[hd * chunk:(hd + 1) * chunk, :] = jnp.where(col_head == hd, vc, jnp.zeros_like(vc))
        s = _dot(sgw_ref[...], vs_ref[...]) + sgb_ref[...]
        zb_ref[ci * chunk:(ci + 1) * chunk, :] = (u_ref[ci * chunk:(ci + 1) * chunk, :].astype(F32) * s).astype(BF16)
    merged = merged + g1_ref[...].astype(F32) * _dot(zb_ref[...], wb_ref[...])

    first = (pl.program_id(0) * tm) % seq == 0
    halo = ch_ref[...].astype(F32) * hh_ref[...].astype(F32)
    z_ref[0:8, :] = jnp.where(first, jnp.zeros_like(halo), halo)
    z_ref[8:, :] = c_ref[...].astype(F32) * h_ref[...].astype(F32)
    conv = (cw_ref[0:1, :] * z_ref[pl.ds(6, tm), :] + cw_ref[1:2, :] * z_ref[pl.ds(7, tm), :]
            + cw_ref[2:3, :] * z_ref[pl.ds(8, tm), :])
    zc = b_ref[...].astype(F32) * conv
    merged = merged + g2_ref[...].astype(F32) * _dot(zc.astype(BF16), wc_ref[...])
    o_ref[...] = merged.astype(o_ref.dtype)


def mixer_tail(ya, p5, gates, glu_w, glu_b, ln_g, ln_b, sg_w, sg_b, conv_w, wa, wb, wc, *, seq, tm):
    m, d_ssm = ya.shape
    d_sg = p5.shape[1] // 5
    d_model = wa.shape[1]
    heads, chunk, _ = sg_w.shape
    tm = _tile(m, tm)
    assert tm % chunk == 0 and seq % tm == 0 and conv_w.shape[0] == 3
    w_low = jnp.where(jnp.tril(jnp.ones((chunk, chunk), dtype=bool))[None], sg_w, 0.0)
    sgw_cat = w_low.transpose(1, 0, 2).reshape(chunk, heads * chunk).astype(BF16)
    sgb_full = jnp.repeat(sg_b.T, d_sg // heads, axis=1)
    seg = lambda k: pl.BlockSpec((tm, d_sg), lambda i: (i, k))
    halo = lambda k: pl.BlockSpec((8, d_sg), lambda i: (jnp.maximum(i * (tm // 8) - 1, 0), k))
    gate = lambda k: pl.BlockSpec((tm, d_model), lambda i: (i, k))
    full = lambda a: pl.BlockSpec(a.shape, lambda i: (0,) * a.ndim)
    consts = [glu_w.astype(BF16), glu_b.reshape(1, -1), ln_g.reshape(1, -1), ln_b.reshape(1, -1),
              sgw_cat, sgb_full, conv_w.reshape(3, -1), wa.astype(BF16), wb.astype(BF16), wc.astype(BF16)]
    return pl.pallas_call(
        functools.partial(_mixer_kernel, tm=tm, seq=seq, chunk=chunk, heads=heads), name="mixer_tail",
        grid=(m // tm,),
        in_specs=[pl.BlockSpec((tm, d_ssm), lambda i: (i, 0)),
                  seg(0), seg(1), seg(2), seg(3), seg(4), halo(3), halo(4),
                  gate(0), gate(1), gate(2)] + [full(a) for a in consts],
        out_specs=pl.BlockSpec((tm, d_model), lambda i: (i, 0)),
        out_shape=jax.ShapeDtypeStruct((m, d_model), BF16),
        scratch_shapes=[pltpu.VMEM((tm + 8, d_sg), F32),
                        pltpu.VMEM((heads * chunk, d_sg), BF16),
                        pltpu.VMEM((tm, d_sg), BF16)],
        compiler_params=_params(("parallel",)),
    )(ya, p5, p5, p5, p5, p5, p5, p5, gates, gates, gates, *consts)


def _post_norm(x, y, sh, sc, g, alpha):
    z = alpha * x + (1.0 + g) * y
    return _layer_norm(z) * (1.0 + sc) + sh


def _mod_specs(tm, seq, d):
    spec = pl.BlockSpec((None, 1, d), lambda i, *_: ((i * tm) // seq, 0, 0))
    return [spec, spec, spec]


def _route_top2(x, rwt_ref, rb_ref, e_ref, w_ref):
    logits = lax.dot_general(rwt_ref[...], x, (((1,), (1,)), ((), ())),
                             preferred_element_type=F32, precision=HIGHEST) + rb_ref[...]
    n_exp = logits.shape[0]
    eid = lax.broadcasted_iota(jnp.int32, logits.shape, 0)
    m1 = jnp.max(logits, axis=0, keepdims=True)
    e1 = jnp.min(jnp.where(logits == m1, eid, n_exp), axis=0, keepdims=True)
    rest = jnp.where(eid == e1, -jnp.inf, logits)
    m2 = jnp.max(rest, axis=0, keepdims=True)
    e2 = jnp.min(jnp.where(rest == m2, eid, n_exp), axis=0, keepdims=True)
    ex = jnp.exp(m2 - m1)
    e_ref[0:1, :] = e1
    e_ref[1:2, :] = e2
    w_ref[0:1, :] = 1.0 / (1.0 + ex)
    w_ref[1:2, :] = ex / (1.0 + ex)


def _outproj_kernel(a_ref, w_ref, x_ref, sh_ref, sc_ref, g_ref, o32_ref, o16_ref, *, alpha):
    y = _dot(a_ref[...], w_ref[...])
    out = _post_norm(x_ref[...], y, sh_ref[...], sc_ref[...], g_ref[...], alpha)
    o32_ref[...] = out
    o16_ref[...] = out.astype(BF16)


def _outproj_route_kernel(a_ref, w_ref, x_ref, sh_ref, sc_ref, g_ref, rwt_ref, rb_ref, o32_ref, e_ref, tw_ref, *, alpha):
    y = _dot(a_ref[...], w_ref[...])
    out = _post_norm(x_ref[...], y, sh_ref[...], sc_ref[...], g_ref[...], alpha)
    o32_ref[...] = out
    _route_top2(out, rwt_ref, rb_ref, e_ref, tw_ref)


def outproj_postnorm(a, w_o, x, sh, sc, g, *, seq, alpha, tm, router=None):
    m, d = x.shape
    tm = _tile(m, tm)
    assert seq % tm == 0
    row = pl.BlockSpec((tm, d), lambda i: (i, 0))
    in_specs = [pl.BlockSpec((tm, a.shape[1]), lambda i: (i, 0)),
                pl.BlockSpec(w_o.shape, lambda i: (0, 0)), row] + _mod_specs(tm, seq, d)
    if router is None:
        return pl.pallas_call(
            functools.partial(_outproj_kernel, alpha=alpha), name="outproj_postnorm",
            grid=(m // tm,),
            in_specs=in_specs,
            out_specs=[row, row],
            out_shape=[jax.ShapeDtypeStruct((m, d), F32), jax.ShapeDtypeStruct((m, d), BF16)],
            compiler_params=_params(("parallel",)),
        )(a, w_o, x, sh, sc, g)
    router_w, router_b = router
    n_exp = router_w.shape[1]
    top = pl.BlockSpec((TOP_K, tm), lambda i: (0, i))
    return pl.pallas_call(
        functools.partial(_outproj_route_kernel, alpha=alpha), name="outproj_postnorm_route",
        grid=(m // tm,),
        in_specs=in_specs + [pl.BlockSpec((n_exp, d), lambda i: (0, 0)), pl.BlockSpec((n_exp, 1), lambda i: (0, 0))],
        out_specs=[row, top, top],
        out_shape=[jax.ShapeDtypeStruct((m, d), F32), jax.ShapeDtypeStruct((TOP_K, m), jnp.int32),
                   jax.ShapeDtypeStruct((TOP_K, m), F32)],
        compiler_params=_params(("parallel",)),
    )(a, w_o, x, sh, sc, g, router_w.T, router_b.reshape(n_exp, 1))


def _swiglu_step(xb, w1_ref, w3_ref, w2_ref, acc_ref):
    gate = _dot(xb, w1_ref[...])
    up = _dot(xb, w3_ref[...])
    act = (gate * jax.nn.sigmoid(gate) * up).astype(BF16)
    acc_ref[...] += _dot(act, w2_ref[...])


def _cast_split(shape, n_blocks):
    e, r, c = shape
    if n_blocks % e:
        return None
    per = n_blocks // e
    for cb in (8, 4, 2, 1):
        if per % cb == 0 and c % (cb * 128) == 0 and r % ((per // cb) * 16) == 0:
            return per // cb, cb
    return None


def _ffn_kernel(xb_ref, w1_ref, w3_ref, w2_ref, x_ref, sh_ref, sc_ref, g_ref, *rest, alpha, n_cast, cast_steps):
    cast_in, (o32_ref, o16_ref), cast_out = rest[:n_cast], rest[n_cast:n_cast + 2], rest[n_cast + 2:]
    f = pl.program_id(1)

    @pl.when(f == 0)
    def _():
        o32_ref[...] = jnp.zeros_like(o32_ref)

    @pl.when(f < cast_steps)
    def _():
        for src, dst in zip(cast_in, cast_out):
            dst[...] = src[...].astype(BF16)

    _swiglu_step(xb_ref[...], w1_ref, w3_ref, w2_ref, o32_ref)

    @pl.when(f == pl.num_programs(1) - 1)
    def _():
        out = _post_norm(x_ref[...], o32_ref[...], sh_ref[...], sc_ref[...], g_ref[...], alpha)
        o32_ref[...] = out
        o16_ref[...] = out.astype(BF16)


def ffn_postnorm(xb, x, w13, w2, sh, sc, g, *, seq, alpha, tm, tf, cast=()):
    m, d = x.shape
    ffn = w2.shape[0]
    tm, tf = _tile(m, tm), _tile(ffn, tf)
    nf = ffn // tf
    assert seq % tm == 0
    cast_steps = 1 << (nf.bit_length() - 1)
    splits = [_cast_split(a.shape, (m // tm) * cast_steps) for a in cast]
    in_kernel = [a for a, sp in zip(cast, splits) if sp is not None]
    cast_specs = []
    for a, (rb, cb) in [(a, sp) for a, sp in zip(cast, splits) if sp is not None]:
        def index(i, f, rb=rb, cb=cb):
            t = i * cast_steps + jnp.minimum(f, cast_steps - 1)
            return t // (rb * cb), (t // cb) % rb, t % cb
        cast_specs.append(pl.BlockSpec((None, a.shape[1] // rb, a.shape[2] // cb), index))
    row = pl.BlockSpec((tm, d), lambda i, f: (i, 0))
    outs = pl.pallas_call(
        functools.partial(_ffn_kernel, alpha=alpha, n_cast=len(in_kernel), cast_steps=cast_steps), name="ffn_postnorm",
        grid=(m // tm, nf),
        in_specs=[row,
                  pl.BlockSpec((d, tf), lambda i, f: (0, f)),
                  pl.BlockSpec((d, tf), lambda i, f: (0, nf + f)),
                  pl.BlockSpec((tf, d), lambda i, f: (f, 0)),
                  row] + _mod_specs(tm, seq, d) + cast_specs,
        out_specs=[row, row] + cast_specs,
        out_shape=[jax.ShapeDtypeStruct((m, d), F32), jax.ShapeDtypeStruct((m, d), BF16)]
                  + [jax.ShapeDtypeStruct(a.shape, BF16) for a in in_kernel],
        compiler_params=_params(("parallel", "arbitrary")),
    )(xb, w13, w13, w2, x, sh, sc, g, *in_kernel)
    done = iter(outs[2:])
    return outs[0], outs[1], [next(done) if sp is not None else a.astype(BF16) for a, sp in zip(cast, splits)]


def _rank_kernel(e_ref, rank_ref, count_ref, carry_ref, *, n_exp):
    @pl.when(pl.program_id(0) == 0)
    def _():
        carry_ref[...] = jnp.zeros_like(carry_ref)

    e = e_ref[...]
    tb = e.shape[1]
    eid = lax.broadcasted_iota(jnp.int32, (n_exp, tb), 0)
    hit = (eid == e[0:1, :]) | (eid == e[1:2, :])
    onehot = hit.astype(F32)
    upper = (lax.broadcasted_iota(jnp.int32, (tb, tb), 0) <= lax.broadcasted_iota(jnp.int32, (tb, tb), 1))
    incl = _dot(hit.astype(BF16), upper.astype(BF16))
    excl = incl - onehot + carry_ref[...]
    r0 = jnp.sum(jnp.where(eid == e[0:1, :], excl, 0.0), axis=0, keepdims=True)
    r1 = jnp.sum(jnp.where(eid == e[1:2, :], excl, 0.0), axis=0, keepdims=True)
    rank_ref[0:1, :] = r0.astype(jnp.int32)
    rank_ref[1:2, :] = r1.astype(jnp.int32)
    carry_ref[...] += jnp.sum(onehot, axis=1, keepdims=True)
    count_ref[...] = carry_ref[...].astype(jnp.int32)


def expert_ranks(e_idx, n_exp, *, tb):
    m = e_idx.shape[1]
    tb = _tile(m, tb)
    blk = pl.BlockSpec((TOP_K, tb), lambda i: (0, i))
    return pl.pallas_call(
        functools.partial(_rank_kernel, n_exp=n_exp), name="expert_ranks",
        grid=(m // tb,),
        in_specs=[blk],
        out_specs=[blk, pl.BlockSpec((n_exp, 1), lambda i: (0, 0))],
        out_shape=[jax.ShapeDtypeStruct((TOP_K, m), jnp.int32), jax.ShapeDtypeStruct((n_exp, 1), jnp.int32)],
        scratch_shapes=[pltpu.VMEM((n_exp, 1), F32)],
        compiler_params=_params(("arbitrary",)),
    )(e_idx)


def _dispatch_kernel(pad_ref, dest_ref, x_ref, xs_ref, zero_ref, sem, zsem):
    tm, d = x_ref.shape
    n_pad = pad_ref.shape[1]

    @pl.when(pl.program_id(0) == 0)
    def _():
        zero_ref[...] = jnp.zeros_like(zero_ref)

        def zero_copy(r):
            return pltpu.make_async_copy(zero_ref, xs_ref.at[pl.ds(r, 1), :], zsem)

        for e in range(n_pad):
            lax.fori_loop(pad_ref[0, e], pad_ref[1, e], lambda r, c: (zero_copy(r).start(), c)[1], 0)
        for e in range(n_pad):
            lax.fori_loop(pad_ref[0, e], pad_ref[1, e], lambda r, c: (zero_copy(r).wait(), c)[1], 0)

    def issue(r, carry):
        for k in range(TOP_K):
            pltpu.make_async_copy(x_ref.at[pl.ds(r, 1), :], xs_ref.at[pl.ds(dest_ref[0, k, r], 1), :], sem).start()
        return carry

    lax.fori_loop(0, tm, issue, 0, unroll=8)
    for k in range(TOP_K):
        pltpu.make_async_copy(x_ref, xs_ref.at[pl.ds(0, tm), :], sem).wait()


def dispatch_rows(x, dest, pad_bounds, n_rows, *, tm):
    m, d = x.shape
    tm = _tile(m, tm)
    dest_t = dest.reshape(TOP_K, m // tm, tm).transpose(1, 0, 2)
    return pl.pallas_call(
        _dispatch_kernel, name="dispatch_rows",
        grid=(m // tm,),
        in_specs=[pl.BlockSpec(memory_space=pltpu.SMEM),
                  pl.BlockSpec((1, TOP_K, tm), lambda i: (i, 0, 0), memory_space=pltpu.SMEM),
                  pl.BlockSpec((tm, d), lambda i: (i, 0))],
        out_specs=pl.BlockSpec(memory_space=pl.ANY),
        out_shape=jax.ShapeDtypeStruct((n_rows, d), x.dtype),
        scratch_shapes=[pltpu.VMEM((1, d), x.dtype), pltpu.SemaphoreType.DMA(()), pltpu.SemaphoreType.DMA(())],
        compiler_params=_params(("arbitrary",)),
    )(pad_bounds, dest_t, x)


def _expert_kernel(be_ref, nv_ref, rb_ref, xs_ref, w1_ref, w3_ref, w2_ref, ys_ref, xb_ref):
    del be_ref, rb_ref
    b, f = pl.program_id(0), pl.program_id(1)
    used = nv_ref[b] > 0

    @pl.when(f == 0)
    def _():
        ys_ref[...] = jnp.zeros_like(ys_ref)

    @pl.when(used & (f == 0))
    def _():
        xb_ref[...] = xs_ref[...].astype(BF16)

    @pl.when(used)
    def _():
        _swiglu_step(xb_ref[...], w1_ref, w3_ref, w2_ref, ys_ref)


def expert_blocks(xs, w13, w2, block_e, block_nv, block_row, *, rows, tf):
    n_rows, d = xs.shape
    n_exp, ffn, _ = w2.shape
    tf = _tile(ffn, tf)
    nf = ffn // tf
    nb = n_rows // rows
    fi = lambda b, f, nv: jnp.where(nv[b] > 0, f, nf - 1)
    grid_spec = pltpu.PrefetchScalarGridSpec(
        num_scalar_prefetch=3,
        grid=(nb, nf),
        in_specs=[pl.BlockSpec((rows, d), lambda b, f, be, nv, rb: (rb[b], 0)),
                  pl.BlockSpec((None, d, tf), lambda b, f, be, nv, rb: (be[b], 0, fi(b, f, nv))),
                  pl.BlockSpec((None, d, tf), lambda b, f, be, nv, rb: (be[b], 0, nf + fi(b, f, nv))),
                  pl.BlockSpec((None, tf, d), lambda b, f, be, nv, rb: (be[b], fi(b, f, nv), 0))],
        out_specs=pl.BlockSpec((rows, d), lambda b, f, be, nv, rb: (b, 0)),
        scratch_shapes=[pltpu.VMEM((rows, d), BF16)])
    return pl.pallas_call(
        _expert_kernel, name="expert_blocks",
        grid_spec=grid_spec,
        out_shape=jax.ShapeDtypeStruct((n_rows, d), F32),
        compiler_params=_params(("arbitrary", "arbitrary")),
    )(block_e, block_nv, block_row, xs, w13, w13, w2)


def _combine_kernel(dest_ref, dnext_ref, ys_ref, w0_ref, w1_ref, x_ref, sh_ref, sc_ref, g_ref, o_ref, buf_ref, sems, *, alpha):
    i, n = pl.program_id(0), pl.num_programs(0)
    tm = x_ref.shape[0]
    slot = i % 2

    def gather(d_ref, s):
        def issue(r, carry):
            for k in range(TOP_K):
                pltpu.make_async_copy(ys_ref.at[pl.ds(d_ref[0, k, r], 1), :],
                                      buf_ref.at[s, k, pl.ds(r, 1), :], sems.at[s]).start()
            return carry
        lax.fori_loop(0, tm, issue, 0, unroll=8)

    @pl.when(i == 0)
    def _():
        gather(dest_ref, 0)

    @pl.when(i + 1 < n)
    def _():
        gather(dnext_ref, 1 - slot)

    for k in range(TOP_K):
        pltpu.make_async_copy(ys_ref.at[pl.ds(0, tm), :], buf_ref.at[slot, k], sems.at[slot]).wait()
    y = w0_ref[...] * buf_ref[slot, 0] + w1_ref[...] * buf_ref[slot, 1]
    o_ref[...] = _post_norm(x_ref[...], y, sh_ref[...], sc_ref[...], g_ref[...], alpha)


def combine_postnorm(ys, dest, top_w, x, sh, sc, g, *, seq, alpha, tm):
    m, d = x.shape
    tm = _tile(m, tm)
    assert seq % tm == 0
    n = m // tm
    dest_t = dest.reshape(TOP_K, n, tm).transpose(1, 0, 2)
    row = pl.BlockSpec((tm, d), lambda i: (i, 0))
    col = pl.BlockSpec((tm, 1), lambda i: (i, 0))
    return pl.pallas_call(
        functools.partial(_combine_kernel, alpha=alpha), name="combine_postnorm",
        grid=(n,),
        in_specs=[pl.BlockSpec((1, TOP_K, tm), lambda i: (i, 0, 0), memory_space=pltpu.SMEM),
                  pl.BlockSpec((1, TOP_K, tm), lambda i: (jnp.minimum(i + 1, n - 1), 0, 0), memory_space=pltpu.SMEM),
                  pl.BlockSpec(memory_space=pl.ANY), col, col, row] + _mod_specs(tm, seq, d),
        out_specs=row,
        out_shape=jax.ShapeDtypeStruct((m, d), F32),
        scratch_shapes=[pltpu.VMEM((2, TOP_K, tm, d), F32), pltpu.SemaphoreType.DMA((2,))],
        compiler_params=_params(("arbitrary",)),
    )(dest_t, dest_t, ys, top_w[0].reshape(m, 1), top_w[1].reshape(m, 1), x, sh, sc, g)


def moe_postnorm(x, e_idx, top_w, w13, w2, sh, sc, g, *, seq, alpha):
    m, d = x.shape
    n_exp = w2.shape[0]
    rank, counts = expert_ranks(e_idx, n_exp, tb=512)
    counts = counts[:, 0]
    padded = ((counts + MOE_ROWS - 1) // MOE_ROWS) * MOE_ROWS
    ends = jnp.cumsum(padded)
    starts = ends - padded
    expert_id = jnp.arange(n_exp, dtype=jnp.int32)[:, None, None]
    dest = rank + jnp.sum(jnp.where(e_idx[None] == expert_id, starts[:, None, None], 0), axis=0)
    dest = dest.astype(jnp.int32)
    n_rows = ((TOP_K * m + MOE_ROWS - 1) // MOE_ROWS) * MOE_ROWS + n_exp * MOE_ROWS
    pad_bounds = jnp.stack([jnp.append(starts + counts, ends[-1]), jnp.append(ends, n_rows)]).astype(jnp.int32)
    nb = n_rows // MOE_ROWS
    block_start = jnp.arange(nb, dtype=jnp.int32) * MOE_ROWS
    last_used = jnp.maximum(ends[-1] // MOE_ROWS - 1, 0)
    block_e = jnp.minimum(jnp.sum(block_start[:, None] >= ends[None, :], axis=1), n_exp - 1).astype(jnp.int32)
    block_e = jnp.where(block_start < ends[-1], block_e, block_e[last_used])
    block_nv = jnp.clip(counts[block_e] - (block_start - starts[block_e]), 0, MOE_ROWS)
    block_nv = jnp.where(block_start < ends[-1], block_nv, 0).astype(jnp.int32)
    block_row = jnp.minimum(jnp.arange(nb, dtype=jnp.int32), last_used).astype(jnp.int32)
    xs = dispatch_rows(x, dest, pad_bounds, n_rows, tm=512)
    ys = expert_blocks(xs, w13, w2, block_e, block_nv, block_row, rows=MOE_ROWS, tf=1024)
    return combine_postnorm(ys, dest, top_w, x, sh, sc, g, seq=seq, alpha=alpha, tm=512)


def kernel(x, c, w_in, ssm_a_re, ssm_a_im, ssm_log_dt, ssm_b_re, ssm_b_im, ssm_c_re, ssm_c_im, ssm_d, glu_w, glu_b, sg_ln_g, sg_ln_b, sg_w, sg_b, conv_w, w_branch_a, w_branch_b, w_branch_c, w_o, ada_w, ada_b, ffn_w13, ffn_w2, moe_router_w, moe_router_b, moe_w13, moe_w2):
    bsz, seq, d = x.shape
    depth = w_in.shape[0]
    m = bsz * seq
    d_ssm = glu_w.shape[1]
    d_sg = sg_ln_g.shape[1]
    alpha = (2.0 * depth) ** 0.25
    assert conv_w.shape[-1] == d_sg and w_in.shape[2] == d_ssm + 5 * d_sg + 3 * d

    mod = ada_modulation(c, ada_w, ada_b)
    ssm_mats = jax.vmap(functools.partial(ssm_matrices, t_chunk=SSM_CHUNK))(
        ssm_a_re, ssm_a_im, ssm_log_dt, ssm_b_re, ssm_b_im, ssm_c_re, ssm_c_im, ssm_d)
    moe_bf16 = None
    x32 = x.reshape(m, d)
    xb = None
    for l in range(depth):
        sh_m, sc_m, g_m, sh_f, sc_f, g_f = [mod[l, :, k * d:(k + 1) * d].reshape(bsz, 1, d) for k in range(6)]
        w_ssm, w_p5, w_gates = [w_in[l, :, a:b].astype(BF16) for a, b in
                                ((0, d_ssm), (d_ssm, d_ssm + 5 * d_sg), (d_ssm + 5 * d_sg, w_in.shape[2]))]
        if xb is None:
            u_ssm, xb = matmul_cast(x32, w_ssm, tm=1024, name="inproj_ssm_cast")
        else:
            u_ssm = matmul(xb, w_ssm, tm=1024, tn=512, name="inproj_ssm")
        p5 = matmul(xb, w_p5, tm=1024, tn=1280, name="inproj_gmlp_conv")
        gates = matmul(xb, w_gates, tm=1024, tn=2048, sigmoid=True, name="inproj_gates")
        y_ssm = ssm_scan(u_ssm, [a[l] for a in ssm_mats], bsz, seq, SSM_CHUNK)
        merged = mixer_tail(y_ssm, p5, gates, glu_w[l], glu_b[l], sg_ln_g[l], sg_ln_b[l], sg_w[l], sg_b[l],
                            conv_w[l], w_branch_a[l], w_branch_b[l], w_branch_c[l], seq=seq, tm=256)
        if l % 2 == 0:
            x32, xb = outproj_postnorm(merged, w_o[l].astype(BF16), x32, sh_m, sc_m, g_m, seq=seq, alpha=alpha, tm=512)
            nxt = [moe_w13[(l + 1) // 2], moe_w2[(l + 1) // 2]] if l + 1 < depth else []
            x32, xb, moe_bf16 = ffn_postnorm(xb, x32, ffn_w13[l // 2].astype(BF16), ffn_w2[l // 2].astype(BF16),
                                             sh_f, sc_f, g_f, seq=seq, alpha=alpha, tm=512, tf=512, cast=nxt)
        else:
            x32, e_idx, top_w = outproj_postnorm(merged, w_o[l].astype(BF16), x32, sh_m, sc_m, g_m, seq=seq,
                                                 alpha=alpha, tm=512, router=(moe_router_w[l // 2], moe_router_b[l // 2]))
            w13_b, w2_b = moe_bf16 or (moe_w13[l // 2].astype(BF16), moe_w2[l // 2].astype(BF16))
            x32 = moe_postnorm(x32, e_idx, top_w, w13_b, w2_b, sh_f, sc_f, g_f, seq=seq, alpha=alpha)
            xb = None
    return x32.reshape(bsz, seq, d)
```

```python
import functools

import jax
import jax.numpy as jnp
from jax import lax
from jax.experimental import pallas as pl
from jax.experimental.pallas import tpu as pltpu

BF16 = jnp.bfloat16
F32 = jnp.float32

LN_EPS = 1e-5
TOP_K = 2
SSM_CHUNK = 32
MOE_ROWS = 512
V7X_VMEM_LIMIT = 56 * 1024 * 1024
HIGHEST = lax.Precision.HIGHEST


def _tile(dim, pref):
    if dim <= pref:
        return dim
    for t in range(pref - pref % 128, 0, -128):
        if dim % t == 0:
            return t
    raise ValueError((dim, pref))


def _params(sem):
    return pltpu.CompilerParams(dimension_semantics=sem, vmem_limit_bytes=V7X_VMEM_LIMIT)


def _dot(a, b):
    return jnp.dot(a, b, preferred_element_type=F32)


def _ada_kernel(c_ref, w_ref, b_ref, o_ref):
    o_ref[...] = jnp.dot(c_ref[...], w_ref[...], preferred_element_type=F32, precision=HIGHEST) + b_ref[...]


def ada_modulation(c, ada_w, ada_b):
    nl, d, d6 = ada_w.shape
    bsz = c.shape[0]
    tn = _tile(d6, 1024)
    return pl.pallas_call(
        _ada_kernel, name="ada_modulation",
        grid=(nl, d6 // tn),
        in_specs=[pl.BlockSpec((bsz, d), lambda l, j: (0, 0)),
                  pl.BlockSpec((None, d, tn), lambda l, j: (l, 0, j)),
                  pl.BlockSpec((None, 1, tn), lambda l, j: (l, 0, j))],
        out_specs=pl.BlockSpec((None, bsz, tn), lambda l, j: (l, 0, j)),
        out_shape=jax.ShapeDtypeStruct((nl, bsz, d6), F32),
        compiler_params=_params(("parallel", "parallel")),
    )(c, ada_w, ada_b.reshape(nl, 1, d6))


def _mm_kernel(a_ref, b_ref, o_ref, *, sigmoid):
    acc = _dot(a_ref[...], b_ref[...])
    if sigmoid:
        acc = jax.nn.sigmoid(acc)
    o_ref[...] = acc.astype(o_ref.dtype)


def matmul(a, b, *, tm, tn, name, out_dtype=BF16, sigmoid=False):
    m, k = a.shape
    n = b.shape[1]
    tm, tn = _tile(m, tm), _tile(n, tn)
    return pl.pallas_call(
        functools.partial(_mm_kernel, sigmoid=sigmoid), name=name,
        grid=(m // tm, n // tn),
        in_specs=[pl.BlockSpec((tm, k), lambda i, j: (i, 0)),
                  pl.BlockSpec((k, tn), lambda i, j: (0, j))],
        out_specs=pl.BlockSpec((tm, tn), lambda i, j: (i, j)),
        out_shape=jax.ShapeDtypeStruct((m, n), out_dtype),
        compiler_params=_params(("parallel", "parallel")),
    )(a, b)


def _mm_cast_kernel(a_ref, b_ref, o_ref, a16_ref):
    a16 = a_ref[...].astype(BF16)
    a16_ref[...] = a16
    o_ref[...] = _dot(a16, b_ref[...]).astype(o_ref.dtype)


def matmul_cast(a, b, *, tm, name):
    m, k = a.shape
    n = b.shape[1]
    tm = _tile(m, tm)
    return pl.pallas_call(
        _mm_cast_kernel, name=name,
        grid=(m // tm,),
        in_specs=[pl.BlockSpec((tm, k), lambda i: (i, 0)),
                  pl.BlockSpec((k, n), lambda i: (0, 0))],
        out_specs=[pl.BlockSpec((tm, n), lambda i: (i, 0)), pl.BlockSpec((tm, k), lambda i: (i, 0))],
        out_shape=[jax.ShapeDtypeStruct((m, n), BF16), jax.ShapeDtypeStruct((m, k), BF16)],
        compiler_params=_params(("parallel",)),
    )(a, b)


def _to_lanes_kernel(u_ref, o_ref, uf_ref, xc_ref, *, t_chunk):
    w, chunks = u_ref.shape[1], o_ref.shape[1]
    for k in range(w // 128):
        uf_ref[k] = u_ref[:, k * 128:(k + 1) * 128].astype(F32)
    for s in range(t_chunk):
        for k in range(w // 128):
            xc_ref[:, s * w + k * 128:s * w + (k + 1) * 128] = uf_ref[k, pl.ds(s, chunks, stride=t_chunk), :]
    o_ref[...] = xc_ref[...].T.astype(o_ref.dtype)


def _from_lanes_kernel(y_ref, o_ref, yc_ref, of_ref, *, t_chunk):
    w, chunks = o_ref.shape[1], y_ref.shape[1]
    yc_ref[...] = y_ref[...].astype(F32).T
    for s in range(t_chunk):
        for k in range(w // 128):
            of_ref[k, pl.ds(s, chunks, stride=t_chunk), :] = yc_ref[:, s * w + k * 128:s * w + (k + 1) * 128]
    for k in range(w // 128):
        o_ref[:, k * 128:(k + 1) * 128] = of_ref[k].astype(o_ref.dtype)


def chunks_to_lanes(u, t_chunk, *, tile):
    m, w = u.shape
    assert w % 128 == 0
    chunks = m // t_chunk
    tc = _tile(chunks, tile)
    return pl.pallas_call(
        functools.partial(_to_lanes_kernel, t_chunk=t_chunk), name="ssm_to_lanes",
        grid=(chunks // tc,),
        in_specs=[pl.BlockSpec((tc * t_chunk, w), lambda i: (i, 0))],
        out_specs=pl.BlockSpec((t_chunk * w, tc), lambda i: (0, i)),
        out_shape=jax.ShapeDtypeStruct((t_chunk * w, chunks), u.dtype),
        scratch_shapes=[pltpu.VMEM((w // 128, tc * t_chunk, 128), F32), pltpu.VMEM((tc, t_chunk * w), F32)],
        compiler_params=_params(("parallel",)),
    )(u)


def lanes_to_chunks(yt, t_chunk, *, tile):
    tw, chunks = yt.shape
    w = tw // t_chunk
    tc = _tile(chunks, tile)
    return pl.pallas_call(
        functools.partial(_from_lanes_kernel, t_chunk=t_chunk), name="ssm_from_lanes",
        grid=(chunks // tc,),
        in_specs=[pl.BlockSpec((tw, tc), lambda i: (0, i))],
        out_specs=pl.BlockSpec((tc * t_chunk, w), lambda i: (i, 0)),
        out_shape=jax.ShapeDtypeStruct((chunks * t_chunk, w), yt.dtype),
        scratch_shapes=[pltpu.VMEM((tc, tw), F32), pltpu.VMEM((w // 128, tc * t_chunk, 128), F32)],
        compiler_params=_params(("parallel",)),
    )(yt)


def ssm_matrices(a_re, a_im, log_dt, b_re, b_im, c_re, c_im, d_skip, t_chunk):
    g, n = a_re.shape
    c = b_re.shape[-1]
    dt = jnp.exp(log_dt)[:, None]
    abar_re = jnp.exp(dt * a_re) * jnp.cos(dt * a_im)
    abar_im = jnp.exp(dt * a_re) * jnp.sin(dt * a_im)
    den = jnp.square(a_re) + jnp.square(a_im)
    num_re = abar_re - 1.0
    f_re = (num_re * a_re + abar_im * a_im) / den
    f_im = (abar_im * a_re - num_re * a_im) / den
    bbar_re = f_re[..., None] * b_re - f_im[..., None] * b_im
    bbar_im = f_re[..., None] * b_im + f_im[..., None] * b_re
    k = jnp.arange(t_chunk + 1, dtype=a_re.dtype)[:, None, None]
    pw_mag = jnp.exp(k * (dt * a_re))
    pw_re = pw_mag * jnp.cos(k * (dt * a_im))
    pw_im = pw_mag * jnp.sin(k * (dt * a_im))
    e_re = pw_re[..., None] * bbar_re - pw_im[..., None] * bbar_im
    e_im = pw_re[..., None] * bbar_im + pw_im[..., None] * bbar_re
    kern = (jnp.einsum('gcn,kgnd->gkcd', c_re, e_re[:t_chunk], precision=HIGHEST)
            - jnp.einsum('gcn,kgnd->gkcd', c_im, e_im[:t_chunk], precision=HIGHEST))
    kern = kern.at[:, 0].add(d_skip[:, :, None] * jnp.eye(c, dtype=kern.dtype))
    k_rev = kern[:, ::-1].transpose(0, 2, 1, 3).reshape(g, c, t_chunk * c)
    p_re = e_re[:t_chunk][::-1].transpose(1, 2, 0, 3).reshape(g, n, t_chunk * c)
    p_im = e_im[:t_chunk][::-1].transpose(1, 2, 0, 3).reshape(g, n, t_chunk * c)
    p_t = jnp.concatenate([p_re, p_im], axis=1)
    q_re = (c_re[None] * pw_re[1:, :, None, :] - c_im[None] * pw_im[1:, :, None, :])
    q_im = -(c_re[None] * pw_im[1:, :, None, :] + c_im[None] * pw_re[1:, :, None, :])
    q_t = jnp.concatenate([q_re, q_im], axis=-1).transpose(1, 0, 2, 3).reshape(g, t_chunk * c, 2 * n)
    lam_re, lam_im = pw_re[t_chunk], pw_im[t_chunk]
    lam = jnp.stack([jnp.concatenate([lam_re, lam_re], axis=-1),
                     jnp.concatenate([-lam_im, lam_im], axis=-1)], axis=1)
    return k_rev, p_t, q_t, lam


def _ssm_kernel(x_ref, krev_ref, pt_ref, qt_ref, lam_ref, o_ref, mt_ref, s_ref, ss_ref, h_ref, *, n_steps, bsz):
    c, tc = krev_ref.shape
    k_rev = krev_ref[...]
    base = jnp.concatenate([k_rev, jnp.zeros_like(k_rev)], axis=1)
    for t in range(tc // c):
        shift = tc - (t + 1) * c
        mt_ref[t * c:(t + 1) * c, :] = base[:, shift:shift + tc].astype(BF16)
    xt = x_ref[...].reshape(tc, x_ref.shape[-1])
    s = _dot(pt_ref[...].astype(F32), xt.astype(F32)).T
    half = s.shape[1] // 2
    s_ref[...] = s
    ss_ref[...] = pltpu.roll(s, half, 1)
    a = lam_ref[0:1, :]
    bm = lam_ref[1:2, :]

    def step(j, carry):
        new = []
        for b in range(bsz):
            h, hs = carry[b]
            row = pl.ds(b * n_steps + j, 1)
            h_ref[row, :] = h
            new.append((a * h + bm * hs + s_ref[row, :], a * hs - bm * h + ss_ref[row, :]))
        return tuple(new)

    zero = jnp.zeros((1, 2 * half), F32)
    lax.fori_loop(0, n_steps, step, tuple((zero, zero) for _ in range(bsz)))
    y = _dot(mt_ref[...], xt)
    y = y + _dot(qt_ref[...], h_ref[...].T.astype(BF16))
    o_ref[...] = y.astype(o_ref.dtype).reshape(o_ref.shape)


def ssm_scan(u, mats, bsz, seq, t_chunk):
    k_rev, p_t, q_t, lam = mats
    g, c, tc = k_rev.shape
    n2 = lam.shape[-1]
    n_steps = seq // t_chunk
    chunks = n_steps * bsz
    xt = chunks_to_lanes(u, t_chunk, tile=128).reshape(t_chunk, g, c, chunks)
    xspec = pl.BlockSpec((t_chunk, None, c, chunks), lambda i: (0, i, 0, 0))
    wspec = lambda shape: pl.BlockSpec((None,) + shape, lambda i: (i, 0, 0))
    yt = pl.pallas_call(
        functools.partial(_ssm_kernel, n_steps=n_steps, bsz=bsz), name="ssm_scan",
        grid=(g,),
        in_specs=[xspec, wspec((c, tc)), wspec((n2, tc)), wspec((tc, n2)), wspec((2, n2))],
        out_specs=xspec,
        out_shape=jax.ShapeDtypeStruct((t_chunk, g, c, chunks), BF16),
        scratch_shapes=[pltpu.VMEM((tc, tc), BF16)] + [pltpu.VMEM((chunks, n2), F32) for _ in range(3)],
        compiler_params=_params(("parallel",)),
    )(xt, k_rev, p_t.astype(BF16), q_t.astype(BF16), lam)
    return lanes_to_chunks(yt.reshape(t_chunk * g * c, chunks), t_chunk, tile=128)


def _layer_norm(z):
    mu = jnp.mean(z, axis=-1, keepdims=True)
    zc = z - mu
    var = jnp.mean(zc * zc, axis=-1, keepdims=True)
    return zc * lax.rsqrt(var + LN_EPS)


def _mixer_kernel(ya_ref, u_ref, v_ref, b_ref, c_ref, h_ref, ch_ref, hh_ref, g0_ref, g1_ref, g2_ref,
                  gluw_ref, glub_ref, lng_ref, lnb_ref, sgw_ref, sgb_ref, cw_ref, wa_ref, wb_ref, wc_ref,
                  o_ref, z_ref, vs_ref, zb_ref, *, tm, seq, chunk, heads):
    z = jax.nn.gelu(ya_ref[...].astype(F32), approximate=True)
    za = z * jax.nn.sigmoid(_dot(z.astype(BF16), gluw_ref[...]) + glub_ref[...])
    merged = g0_ref[...].astype(F32) * _dot(za.astype(BF16), wa_ref[...])

    d_sg = v_ref.shape[-1]
    head_dim = d_sg // heads
    v = (_layer_norm(v_ref[...].astype(F32)) * lng_ref[...] + lnb_ref[...]).astype(BF16)
    col_head = lax.broadcasted_iota(jnp.int32, (chunk, d_sg), 1) // head_dim
    for ci in range(tm // chunk):
        vc = v[ci * chunk:(ci + 1) * chunk, :]
        for hd in range(heads):
            vs_ref[hd * chunk:(hd + 1) * chunk, :] = jnp.where(col_head == hd, vc, jnp.zeros_like(vc))
        s = _dot(sgw_ref[...], vs_ref[...]) + sgb_ref[...]
        zb_ref[ci * chunk:(ci + 1) * chunk, :] = (u_ref[ci * chunk:(ci + 1) * chunk, :].astype(F32) * s).astype(BF16)
    merged = merged + g1_ref[...].astype(F32) * _dot(zb_ref[...], wb_ref[...])

    first = (pl.program_id(0) * tm) % seq == 0
    halo = ch_ref[...].astype(F32) * hh_ref[...].astype(F32)
    z_ref[0:8, :] = jnp.where(first, jnp.zeros_like(halo), halo)
    z_ref[8:, :] = c_ref[...].astype(F32) * h_ref[...].astype(F32)
    conv = (cw_ref[0:1, :] * z_ref[pl.ds(6, tm), :] + cw_ref[1:2, :] * z_ref[pl.ds(7, tm), :]
            + cw_ref[2:3, :] * z_ref[pl.ds(8, tm), :])
    zc = b_ref[...].astype(F32) * conv
    merged = merged + g2_ref[...].astype(F32) * _dot(zc.astype(BF16), wc_ref[...])
    o_ref[...] = merged.astype(o_ref.dtype)


def mixer_tail(ya, p5, gates, glu_w, glu_b, ln_g, ln_b, sg_w, sg_b, conv_w, wa, wb, wc, *, seq, tm):
    m, d_ssm = ya.shape
    d_sg = p5.shape[1] // 5
    d_model = wa.shape[1]
    heads, chunk, _ = sg_w.shape
    tm = _tile(m, tm)
    assert tm % chunk == 0 and seq % tm == 0 and conv_w.shape[0] == 3
    w_low = jnp.where(jnp.tril(jnp.ones((chunk, chunk), dtype=bool))[None], sg_w, 0.0)
    sgw_cat = w_low.transpose(1, 0, 2).reshape(chunk, heads * chunk).astype(BF16)
    sgb_full = jnp.repeat(sg_b.T, d_sg // heads, axis=1)
    seg = lambda k: pl.BlockSpec((tm, d_sg), lambda i: (i, k))
    halo = lambda k: pl.BlockSpec((8, d_sg), lambda i: (jnp.maximum(i * (tm // 8) - 1, 0), k))
    gate = lambda k: pl.BlockSpec((tm, d_model), lambda i: (i, k))
    full = lambda a: pl.BlockSpec(a.shape, lambda i: (0,) * a.ndim)
    consts = [glu_w.astype(BF16), glu_b.reshape(1, -1), ln_g.reshape(1, -1), ln_b.reshape(1, -1),
              sgw_cat, sgb_full, conv_w.reshape(3, -1), wa.astype(BF16), wb.astype(BF16), wc.astype(BF16)]
    return pl.pallas_call(
        functools.partial(_mixer_kernel, tm=tm, seq=seq, chunk=chunk, heads=heads), name="mixer_tail",
        grid=(m // tm,),
        in_specs=[pl.BlockSpec((tm, d_ssm), lambda i: (i, 0)),
                  seg(0), seg(1), seg(2), seg(3), seg(4), halo(3), halo(4),
                  gate(0), gate(1), gate(2)] + [full(a) for a in consts],
        out_specs=pl.BlockSpec((tm, d_model), lambda i: (i, 0)),
        out_shape=jax.ShapeDtypeStruct((m, d_model), BF16),
        scratch_shapes=[pltpu.VMEM((tm + 8, d_sg), F32),
                        pltpu.VMEM((heads * chunk, d_sg), BF16),
                        pltpu.VMEM((tm, d_sg), BF16)],
        compiler_params=_params(("parallel",)),
    )(ya, p5, p5, p5, p5, p5, p5, p5, gates, gates, gates, *consts)


def _post_norm(x, y, sh, sc, g, alpha):
    z = alpha * x + (1.0 + g) * y
    return _layer_norm(z) * (1.0 + sc) + sh


def _mod_specs(tm, seq, d):
    spec = pl.BlockSpec((None, 1, d), lambda i, *_: ((i * tm) // seq, 0, 0))
    return [spec, spec, spec]


def _route_top2(x, rwt_ref, rb_ref, e_ref, w_ref):
    logits = lax.dot_general(rwt_ref[...], x, (((1,), (1,)), ((), ())),
                             preferred_element_type=F32, precision=HIGHEST) + rb_ref[...]
    n_exp = logits.shape[0]
    eid = lax.broadcasted_iota(jnp.int32, logits.shape, 0)
    m1 = jnp.max(logits, axis=0, keepdims=True)
    e1 = jnp.min(jnp.where(logits == m1, eid, n_exp), axis=0, keepdims=True)
    rest = jnp.where(eid == e1, -jnp.inf, logits)
    m2 = jnp.max(rest, axis=0, keepdims=True)
    e2 = jnp.min(jnp.where(rest == m2, eid, n_exp), axis=0, keepdims=True)
    ex = jnp.exp(m2 - m1)
    e_ref[0:1, :] = e1
    e_ref[1:2, :] = e2
    w_ref[0:1, :] = 1.0 / (1.0 + ex)
    w_ref[1:2, :] = ex / (1.0 + ex)


def _outproj_kernel(a_ref, w_ref, x_ref, sh_ref, sc_ref, g_ref, o32_ref, o16_ref, *, alpha):
    y = _dot(a_ref[...], w_ref[...])
    out = _post_norm(x_ref[...], y, sh_ref[...], sc_ref[...], g_ref[...], alpha)
    o32_ref[...] = out
    o16_ref[...] = out.astype(BF16)


def _outproj_route_kernel(a_ref, w_ref, x_ref, sh_ref, sc_ref, g_ref, rwt_ref, rb_ref, o32_ref, e_ref, tw_ref, *, alpha):
    y = _dot(a_ref[...], w_ref[...])
    out = _post_norm(x_ref[...], y, sh_ref[...], sc_ref[...], g_ref[...], alpha)
    o32_ref[...] = out
    _route_top2(out, rwt_ref, rb_ref, e_ref, tw_ref)


def outproj_postnorm(a, w_o, x, sh, sc, g, *, seq, alpha, tm, router=None):
    m, d = x.shape
    tm = _tile(m, tm)
    assert seq % tm == 0
    row = pl.BlockSpec((tm, d), lambda i: (i, 0))
    in_specs = [pl.BlockSpec((tm, a.shape[1]), lambda i: (i, 0)),
                pl.BlockSpec(w_o.shape, lambda i: (0, 0)), row] + _mod_specs(tm, seq, d)
    if router is None:
        return pl.pallas_call(
            functools.partial(_outproj_kernel, alpha=alpha), name="outproj_postnorm",
            grid=(m // tm,),
            in_specs=in_specs,
            out_specs=[row, row],
            out_shape=[jax.ShapeDtypeStruct((m, d), F32), jax.ShapeDtypeStruct((m, d), BF16)],
            compiler_params=_params(("parallel",)),
        )(a, w_o, x, sh, sc, g)
    router_w, router_b = router
    n_exp = router_w.shape[1]
    top = pl.BlockSpec((TOP_K, tm), lambda i: (0, i))
    return pl.pallas_call(
        functools.partial(_outproj_route_kernel, alpha=alpha), name="outproj_postnorm_route",
        grid=(m // tm,),
        in_specs=in_specs + [pl.BlockSpec((n_exp, d), lambda i: (0, 0)), pl.BlockSpec((n_exp, 1), lambda i: (0, 0))],
        out_specs=[row, top, top],
        out_shape=[jax.ShapeDtypeStruct((m, d), F32), jax.ShapeDtypeStruct((TOP_K, m), jnp.int32),
                   jax.ShapeDtypeStruct((TOP_K, m), F32)],
        compiler_params=_params(("parallel",)),
    )(a, w_o, x, sh, sc, g, router_w.T, router_b.reshape(n_exp, 1))


def _swiglu_step(xb, w1_ref, w3_ref, w2_ref, acc_ref):
    gate = _dot(xb, w1_ref[...])
    up = _dot(xb, w3_ref[...])
    act = (gate * jax.nn.sigmoid(gate) * up).astype(BF16)
    acc_ref[...] += _dot(act, w2_ref[...])


def _cast_split(shape, n_blocks):
    e, r, c = shape
    if n_blocks % e:
        return None
    per = n_blocks // e
    for cb in (8, 4, 2, 1):
        if per % cb == 0 and c % (cb * 128) == 0 and r % ((per // cb) * 16) == 0:
            return per // cb, cb
    return None


def _ffn_kernel(xb_ref, w1_ref, w3_ref, w2_ref, x_ref, sh_ref, sc_ref, g_ref, *rest, alpha, n_cast, cast_steps):
    cast_in, (o32_ref, o16_ref), cast_out = rest[:n_cast], rest[n_cast:n_cast + 2], rest[n_cast + 2:]
    f = pl.program_id(1)

    @pl.when(f == 0)
    def _():
        o32_ref[...] = jnp.zeros_like(o32_ref)

    @pl.when(f < cast_steps)
    def _():
        for src, dst in zip(cast_in, cast_out):
            dst[...] = src[...].astype(BF16)

    _swiglu_step(xb_ref[...], w1_ref, w3_ref, w2_ref, o32_ref)

    @pl.when(f == pl.num_programs(1) - 1)
    def _():
        out = _post_norm(x_ref[...], o32_ref[...], sh_ref[...], sc_ref[...], g_ref[...], alpha)
        o32_ref[...] = out
        o16_ref[...] = out.astype(BF16)


def ffn_postnorm(xb, x, w13, w2, sh, sc, g, *, seq, alpha, tm, tf, cast=()):
    m, d = x.shape
    ffn = w2.shape[0]
    tm, tf = _tile(m, tm), _tile(ffn, tf)
    nf = ffn // tf
    assert seq % tm == 0
    cast_steps = 1 << (nf.bit_length() - 1)
    splits = [_cast_split(a.shape, (m // tm) * cast_steps) for a in cast]
    in_kernel = [a for a, sp in zip(cast, splits) if sp is not None]
    cast_specs = []
    for a, (rb, cb) in [(a, sp) for a, sp in zip(cast, splits) if sp is not None]:
        def index(i, f, rb=rb, cb=cb):
            t = i * cast_steps + jnp.minimum(f, cast_steps - 1)
            return t // (rb * cb), (t // cb) % rb, t % cb
        cast_specs.append(pl.BlockSpec((None, a.shape[1] // rb, a.shape[2] // cb), index))
    row = pl.BlockSpec((tm, d), lambda i, f: (i, 0))
    outs = pl.pallas_call(
        functools.partial(_ffn_kernel, alpha=alpha, n_cast=len(in_kernel), cast_steps=cast_steps), name="ffn_postnorm",
        grid=(m // tm, nf),
        in_specs=[row,
                  pl.BlockSpec((d, tf), lambda i, f: (0, f)),
                  pl.BlockSpec((d, tf), lambda i, f: (0, nf + f)),
                  pl.BlockSpec((tf, d), lambda i, f: (f, 0)),
                  row] + _mod_specs(tm, seq, d) + cast_specs,
        out_specs=[row, row] + cast_specs,
        out_shape=[jax.ShapeDtypeStruct((m, d), F32), jax.ShapeDtypeStruct((m, d), BF16)]
                  + [jax.ShapeDtypeStruct(a.shape, BF16) for a in in_kernel],
        compiler_params=_params(("parallel", "arbitrary")),
    )(xb, w13, w13, w2, x, sh, sc, g, *in_kernel)
    done = iter(outs[2:])
    return outs[0], outs[1], [next(done) if sp is not None else a.astype(BF16) for a, sp in zip(cast, splits)]


def _rank_kernel(e_ref, rank_ref, count_ref, carry_ref, *, n_exp):
    @pl.when(pl.program_id(0) == 0)
    def _():
        carry_ref[...] = jnp.zeros_like(carry_ref)

    e = e_ref[...]
    tb = e.shape[1]
    eid = lax.broadcasted_iota(jnp.int32, (n_exp, tb), 0)
    hit = (eid == e[0:1, :]) | (eid == e[1:2, :])
    onehot = hit.astype(F32)
    upper = (lax.broadcasted_iota(jnp.int32, (tb, tb), 0) <= lax.broadcasted_iota(jnp.int32, (tb, tb), 1))
    incl = _dot(hit.astype(BF16), upper.astype(BF16))
    excl = incl - onehot + carry_ref[...]
    r0 = jnp.sum(jnp.where(eid == e[0:1, :], excl, 0.0), axis=0, keepdims=True)
    r1 = jnp.sum(jnp.where(eid == e[1:2, :], excl, 0.0), axis=0, keepdims=True)
    rank_ref[0:1, :] = r0.astype(jnp.int32)
    rank_ref[1:2, :] = r1.astype(jnp.int32)
    carry_ref[...] += jnp.sum(onehot, axis=1, keepdims=True)
    count_ref[...] = carry_ref[...].astype(jnp.int32)


def expert_ranks(e_idx, n_exp, *, tb):
    m = e_idx.shape[1]
    tb = _tile(m, tb)
    blk = pl.BlockSpec((TOP_K, tb), lambda i: (0, i))
    return pl.pallas_call(
        functools.partial(_rank_kernel, n_exp=n_exp), name="expert_ranks",
        grid=(m // tb,),
        in_specs=[blk],
        out_specs=[blk, pl.BlockSpec((n_exp, 1), lambda i: (0, 0))],
        out_shape=[jax.ShapeDtypeStruct((TOP_K, m), jnp.int32), jax.ShapeDtypeStruct((n_exp, 1), jnp.int32)],
        scratch_shapes=[pltpu.VMEM((n_exp, 1), F32)],
        compiler_params=_params(("arbitrary",)),
    )(e_idx)


def _dispatch_kernel(pad_ref, dest_ref, x_ref, xs_ref, zero_ref, sem, zsem):
    tm, d = x_ref.shape
    n_pad = pad_ref.shape[1]

    @pl.when(pl.program_id(0) == 0)
    def _():
        zero_ref[...] = jnp.zeros_like(zero_ref)

        def zero_copy(r):
            return pltpu.make_async_copy(zero_ref, xs_ref.at[pl.ds(r, 1), :], zsem)

        for e in range(n_pad):
            lax.fori_loop(pad_ref[0, e], pad_ref[1, e], lambda r, c: (zero_copy(r).start(), c)[1], 0)
        for e in range(n_pad):
            lax.fori_loop(pad_ref[0, e], pad_ref[1, e], lambda r, c: (zero_copy(r).wait(), c)[1], 0)

    def issue(r, carry):
        for k in range(TOP_K):
            pltpu.make_async_copy(x_ref.at[pl.ds(r, 1), :], xs_ref.at[pl.ds(dest_ref[0, k, r], 1), :], sem).start(priority=k % 2)
        return carry

    lax.fori_loop(0, tm, issue, 0, unroll=8)
    for k in range(TOP_K):
        pltpu.make_async_copy(x_ref, xs_ref.at[pl.ds(0, tm), :], sem).wait()


def dispatch_rows(x, dest, pad_bounds, n_rows, *, tm):
    m, d = x.shape
    tm = _tile(m, tm)
    dest_t = dest.reshape(TOP_K, m // tm, tm).transpose(1, 0, 2)
    return pl.pallas_call(
        _dispatch_kernel, name="dispatch_rows",
        grid=(m // tm,),
        in_specs=[pl.BlockSpec(memory_space=pltpu.SMEM),
                  pl.BlockSpec((1, TOP_K, tm), lambda i: (i, 0, 0), memory_space=pltpu.SMEM),
                  pl.BlockSpec((tm, d), lambda i: (i, 0))],
        out_specs=pl.BlockSpec(memory_space=pl.ANY),
        out_shape=jax.ShapeDtypeStruct((n_rows, d), x.dtype),
        scratch_shapes=[pltpu.VMEM((1, d), x.dtype), pltpu.SemaphoreType.DMA(()), pltpu.SemaphoreType.DMA(())],
        compiler_params=_params(("arbitrary",)),
    )(pad_bounds, dest_t, x)


def _expert_kernel(be_ref, nv_ref, rb_ref, xs_ref, w1_ref, w3_ref, w2_ref, ys_ref, xb_ref):
    del be_ref, rb_ref
    b, f = pl.program_id(0), pl.program_id(1)
    used = nv_ref[b] > 0

    @pl.when(f == 0)
    def _():
        ys_ref[...] = jnp.zeros_like(ys_ref)

    @pl.when(used & (f == 0))
    def _():
        xb_ref[...] = xs_ref[...].astype(BF16)

    @pl.when(used)
    def _():
        _swiglu_step(xb_ref[...], w1_ref, w3_ref, w2_ref, ys_ref)


def expert_blocks(xs, w13, w2, block_e, block_nv, block_row, *, rows, tf):
    n_rows, d = xs.shape
    n_exp, ffn, _ = w2.shape
    tf = _tile(ffn, tf)
    nf = ffn // tf
    nb = n_rows // rows
    fi = lambda b, f, nv: jnp.where(nv[b] > 0, f, nf - 1)
    grid_spec = pltpu.PrefetchScalarGridSpec(
        num_scalar_prefetch=3,
        grid=(nb, nf),
        in_specs=[pl.BlockSpec((rows, d), lambda b, f, be, nv, rb: (rb[b], 0)),
                  pl.BlockSpec((None, d, tf), lambda b, f, be, nv, rb: (be[b], 0, fi(b, f, nv))),
                  pl.BlockSpec((None, d, tf), lambda b, f, be, nv, rb: (be[b], 0, nf + fi(b, f, nv))),
                  pl.BlockSpec((None, tf, d), lambda b, f, be, nv, rb: (be[b], fi(b, f, nv), 0))],
        out_specs=pl.BlockSpec((rows, d), lambda b, f, be, nv, rb: (b, 0)),
        scratch_shapes=[pltpu.VMEM((rows, d), BF16)])
    return pl.pallas_call(
        _expert_kernel, name="expert_blocks",
        grid_spec=grid_spec,
        out_shape=jax.ShapeDtypeStruct((n_rows, d), F32),
        compiler_params=_params(("arbitrary", "arbitrary")),
    )(block_e, block_nv, block_row, xs, w13, w13, w2)


def _combine_kernel(dest_ref, dnext_ref, ys_ref, w0_ref, w1_ref, x_ref, sh_ref, sc_ref, g_ref, o_ref, buf_ref, sems, *, alpha):
    i, n = pl.program_id(0), pl.num_programs(0)
    tm = x_ref.shape[0]
    slot = i % 2

    def gather(d_ref, s):
        def issue(r, carry):
            for k in range(TOP_K):
                pltpu.make_async_copy(ys_ref.at[pl.ds(d_ref[0, k, r], 1), :],
                                      buf_ref.at[s, k, pl.ds(r, 1), :], sems.at[s]).start(priority=k % 2)
            return carry
        lax.fori_loop(0, tm, issue, 0, unroll=8)

    @pl.when(i == 0)
    def _():
        gather(dest_ref, 0)

    @pl.when(i + 1 < n)
    def _():
        gather(dnext_ref, 1 - slot)

    for k in range(TOP_K):
        pltpu.make_async_copy(ys_ref.at[pl.ds(0, tm), :], buf_ref.at[slot, k], sems.at[slot]).wait()
    y = w0_ref[...] * buf_ref[slot, 0] + w1_ref[...] * buf_ref[slot, 1]
    o_ref[...] = _post_norm(x_ref[...], y, sh_ref[...], sc_ref[...], g_ref[...], alpha)


def combine_postnorm(ys, dest, top_w, x, sh, sc, g, *, seq, alpha, tm):
    m, d = x.shape
    tm = _tile(m, tm)
    assert seq % tm == 0
    n = m // tm
    dest_t = dest.reshape(TOP_K, n, tm).transpose(1, 0, 2)
    row = pl.BlockSpec((tm, d), lambda i: (i, 0))
    col = pl.BlockSpec((tm, 1), lambda i: (i, 0))
    return pl.pallas_call(
        functools.partial(_combine_kernel, alpha=alpha), name="combine_postnorm",
        grid=(n,),
        in_specs=[pl.BlockSpec((1, TOP_K, tm), lambda i: (i, 0, 0), memory_space=pltpu.SMEM),
                  pl.BlockSpec((1, TOP_K, tm), lambda i: (jnp.minimum(i + 1, n - 1), 0, 0), memory_space=pltpu.SMEM),
                  pl.BlockSpec(memory_space=pl.ANY), col, col, row] + _mod_specs(tm, seq, d),
        out_specs=row,
        out_shape=jax.ShapeDtypeStruct((m, d), F32),
        scratch_shapes=[pltpu.VMEM((2, TOP_K, tm, d), F32), pltpu.SemaphoreType.DMA((2,))],
        compiler_params=_params(("arbitrary",)),
    )(dest_t, dest_t, ys, top_w[0].reshape(m, 1), top_w[1].reshape(m, 1), x, sh, sc, g)


def moe_postnorm(x, e_idx, top_w, w13, w2, sh, sc, g, *, seq, alpha):
    m, d = x.shape
    n_exp = w2.shape[0]
    rank, counts = expert_ranks(e_idx, n_exp, tb=512)
    counts = counts[:, 0]
    padded = ((counts + MOE_ROWS - 1) // MOE_ROWS) * MOE_ROWS
    ends = jnp.cumsum(padded)
    starts = ends - padded
    expert_id = jnp.arange(n_exp, dtype=jnp.int32)[:, None, None]
    dest = rank + jnp.sum(jnp.where(e_idx[None] == expert_id, starts[:, None, None], 0), axis=0)
    dest = dest.astype(jnp.int32)
    n_rows = ((TOP_K * m + MOE_ROWS - 1) // MOE_ROWS) * MOE_ROWS + n_exp * MOE_ROWS
    pad_bounds = jnp.stack([jnp.append(starts + counts, ends[-1]), jnp.append(ends, n_rows)]).astype(jnp.int32)
    nb = n_rows // MOE_ROWS
    block_start = jnp.arange(nb, dtype=jnp.int32) * MOE_ROWS
    last_used = jnp.maximum(ends[-1] // MOE_ROWS - 1, 0)
    block_e = jnp.minimum(jnp.sum(block_start[:, None] >= ends[None, :], axis=1), n_exp - 1).astype(jnp.int32)
    block_e = jnp.where(block_start < ends[-1], block_e, block_e[last_used])
    block_nv = jnp.clip(counts[block_e] - (block_start - starts[block_e]), 0, MOE_ROWS)
    block_nv = jnp.where(block_start < ends[-1], block_nv, 0).astype(jnp.int32)
    block_row = jnp.minimum(jnp.arange(nb, dtype=jnp.int32), last_used).astype(jnp.int32)
    xs = dispatch_rows(x, dest, pad_bounds, n_rows, tm=512)
    ys = expert_blocks(xs, w13, w2, block_e, block_nv, block_row, rows=MOE_ROWS, tf=1024)
    return combine_postnorm(ys, dest, top_w, x, sh, sc, g, seq=seq, alpha=alpha, tm=512)


def kernel(x, c, w_in, ssm_a_re, ssm_a_im, ssm_log_dt, ssm_b_re, ssm_b_im, ssm_c_re, ssm_c_im, ssm_d, glu_w, glu_b, sg_ln_g, sg_ln_b, sg_w, sg_b, conv_w, w_branch_a, w_branch_b, w_branch_c, w_o, ada_w, ada_b, ffn_w13, ffn_w2, moe_router_w, moe_router_b, moe_w13, moe_w2):
    bsz, seq, d = x.shape
    depth = w_in.shape[0]
    m = bsz * seq
    d_ssm = glu_w.shape[1]
    d_sg = sg_ln_g.shape[1]
    alpha = (2.0 * depth) ** 0.25
    assert conv_w.shape[-1] == d_sg and w_in.shape[2] == d_ssm + 5 * d_sg + 3 * d

    mod = ada_modulation(c, ada_w, ada_b)
    ssm_mats = jax.vmap(functools.partial(ssm_matrices, t_chunk=SSM_CHUNK))(
        ssm_a_re, ssm_a_im, ssm_log_dt, ssm_b_re, ssm_b_im, ssm_c_re, ssm_c_im, ssm_d)
    moe_bf16 = None
    x32 = x.reshape(m, d)
    xb = None
    for l in range(depth):
        sh_m, sc_m, g_m, sh_f, sc_f, g_f = [mod[l, :, k * d:(k + 1) * d].reshape(bsz, 1, d) for k in range(6)]
        w_ssm, w_p5, w_gates = [w_in[l, :, a:b].astype(BF16) for a, b in
                                ((0, d_ssm), (d_ssm, d_ssm + 5 * d_sg), (d_ssm + 5 * d_sg, w_in.shape[2]))]
        if xb is None:
            u_ssm, xb = matmul_cast(x32, w_ssm, tm=1024, name="inproj_ssm_cast")
        else:
            u_ssm = matmul(xb, w_ssm, tm=1024, tn=512, name="inproj_ssm")
        p5 = matmul(xb, w_p5, tm=1024, tn=1280, name="inproj_gmlp_conv")
        gates = matmul(xb, w_gates, tm=1024, tn=2048, sigmoid=True, name="inproj_gates")
        y_ssm = ssm_scan(u_ssm, [a[l] for a in ssm_mats], bsz, seq, SSM_CHUNK)
        merged = mixer_tail(y_ssm, p5, gates, glu_w[l], glu_b[l], sg_ln_g[l], sg_ln_b[l], sg_w[l], sg_b[l],
                            conv_w[l], w_branch_a[l], w_branch_b[l], w_branch_c[l], seq=seq, tm=256)
        if l % 2 == 0:
            x32, xb = outproj_postnorm(merged, w_o[l].astype(BF16), x32, sh_m, sc_m, g_m, seq=seq, alpha=alpha, tm=512)
            nxt = [moe_w13[(l + 1) // 2], moe_w2[(l + 1) // 2]] if l + 1 < depth else []
            x32, xb, moe_bf16 = ffn_postnorm(xb, x32, ffn_w13[l // 2].astype(BF16), ffn_w2[l // 2].astype(BF16),
                                             sh_f, sc_f, g_f, seq=seq, alpha=alpha, tm=512, tf=512, cast=nxt)
        else:
            x32, e_idx, top_w = outproj_postnorm(merged, w_o[l].astype(BF16), x32, sh_m, sc_m, g_m, seq=seq,
                                                 alpha=alpha, tm=512, router=(moe_router_w[l // 2], moe_router_b[l // 2]))
            w13_b, w2_b = moe_bf16 or (moe_w13[l // 2].astype(BF16), moe_w2[l // 2].astype(BF16))
            x32 = moe_postnorm(x32, e_idx, top_w, w13_b, w2_b, sh_f, sc_f, g_f, seq=seq, alpha=alpha)
            xb = None
    return x32.reshape(bsz, seq, d)
```
